```python
import math
import jax, jax.numpy as jnp
from jax import lax
import numpy as np

D_MODEL = 1024
BATCH = 8
SEQ = 2048
DEPTH = 4
DEC_BATCH = 8
DEC_SEQ = 64
PAST_LEN = 1024

CHUNK = 64
N_MEM = 256
HEAD_DIM = 64
SB_HEADS = 8
SB_WIDTH = SB_HEADS * HEAD_DIM
CONV_WIDTH = 256
CONV_K = 3
HG_HEADS = 4
HG_WIDTH = HG_HEADS * HEAD_DIM
MIX_WIDTH = SB_WIDTH + CONV_WIDTH + HG_WIDTH
MEM_HEADS = 4
MEM_HEAD_DIM = D_MODEL // MEM_HEADS
D_FF = 2816
Q_BLOCK = 128
N_SUBLAYERS = 4
EPS = 1e-6
F_FLOOR = 1e-30
IN_SPLITS = (SB_WIDTH,) * 3 + (CONV_WIDTH,) * 3 + (HG_WIDTH,) * 4
IN_WIDTH = sum(IN_SPLITS)
IN_OFFSETS = tuple(int(o) for o in np.cumsum(IN_SPLITS)[:-1])

kernel_name = "hybrid_stream_sb_conv_hgrn2_step"


def rmsnorm(x, g):
    xf = x.astype(jnp.float32)
    y = xf * lax.rsqrt(jnp.mean(xf * xf, axis=-1, keepdims=True) + EPS)
    return (y * g.astype(jnp.float32)).astype(x.dtype)


def swiglu(h, w_gu, w_down):
    gate, up = jnp.split(h @ w_gu, 2, axis=-1)
    return (jax.nn.silu(gate) * up) @ w_down


def _sb_block(q, k, v, q_pos0):
    scale = 1.0 / math.sqrt(q.shape[-1])
    z = jnp.einsum('bthd,bshd->bhts', q, k).astype(jnp.float32) * scale
    t_pos = q_pos0 + jnp.arange(q.shape[1])
    s_pos = jnp.arange(k.shape[1])
    mask = s_pos[None, :] < t_pos[:, None]
    log_keep = jnp.where(mask, jax.nn.log_sigmoid(-z), 0.0)
    log_rest = lax.cumsum(log_keep, axis=3, reverse=True) - log_keep
    w = jnp.where(mask, jnp.exp(jnp.where(mask, jax.nn.log_sigmoid(z) + log_rest, 0.0)), 0.0)
    return jnp.einsum('bhts,bshd->bthd', w.astype(v.dtype), v)


def stick_breaking_attention(q, k, v, q_pos0):
    tq = q.shape[1]
    outs = []
    for start in range(0, tq, Q_BLOCK):
        stop = min(start + Q_BLOCK, tq)
        k_end = q_pos0 + stop
        outs.append(_sb_block(q[:, start:stop], k[:, :k_end], v[:, :k_end], q_pos0 + start))
    return jnp.concatenate(outs, axis=1) if len(outs) > 1 else outs[0]


def causal_short_conv(u, past, w):
    t = u.shape[1]
    full = jnp.concatenate([past, u], axis=1)
    y = full[:, 0:t] * w[0]
    for j in range(1, CONV_K):
        y = y + full[:, j:j + t] * w[j]
    return y, full[:, t:]


def hgrn_lower_bounds(logits):
    p = jax.nn.softmax(logits.astype(jnp.float32), axis=0)
    return jnp.cumsum(p, axis=0) - p[0:1]


def hgrn2_recurrence(q, log_f, k, v, s0):
    b_sz, t, h, _ = q.shape
    dv = v.shape[-1]
    pad = (-t) % CHUNK
    n = (t + pad) // CHUNK

    def blocks(a):
        a = jnp.pad(a.astype(jnp.float32), ((0, 0), (0, pad), (0, 0), (0, 0)))
        return a.reshape(b_sz, n, CHUNK, h, a.shape[-1]).transpose(1, 0, 3, 2, 4)

    causal = jnp.tril(jnp.ones((CHUNK, CHUNK), dtype=bool))[:, :, None]

    def step(s, blk):
        qc, gc, kc, vc = blk
        b = jnp.cumsum(gc, axis=2)
        inter = jnp.einsum('bhtk,bhkv->bhtv', qc * jnp.exp(b), s)
        diff = b[:, :, :, None, :] - b[:, :, None, :, :]
        decay = jnp.where(causal, jnp.exp(jnp.where(causal, diff, 0.0)), 0.0)
        scores = jnp.einsum('bhtsk,bhsk->bhts', qc[:, :, :, None, :] * decay, kc)
        o = inter + jnp.einsum('bhts,bhsv->bhtv', scores, vc)
        b_last = b[:, :, -1:, :]
        s = jnp.exp(b_last[:, :, 0, :])[..., None] * s + jnp.einsum('bhsk,bhsv->bhkv', kc * jnp.exp(b_last - b), vc)
        return s, o

    s, o = lax.scan(step, s0.astype(jnp.float32), (blocks(q), blocks(log_f), blocks(k), blocks(v)))
    o = o.transpose(1, 0, 3, 2, 4).reshape(b_sz, n * CHUNK, h, dv)[:, :t]
    return o.astype(v.dtype), s


def memory_kv(mem, g, w_k, w_v):
    b_sz, n, _ = mem.shape
    m = rmsnorm(mem, g)
    k = (m @ w_k).reshape(b_sz, n, MEM_HEADS, MEM_HEAD_DIM)
    v = (m @ w_v).reshape(b_sz, n, MEM_HEADS, MEM_HEAD_DIM)
    return k, v


def memory_attention(h, mem_k, mem_v, w_q, w_o):
    b_sz, t, _ = h.shape
    q = (h @ w_q).reshape(b_sz, t, MEM_HEADS, MEM_HEAD_DIM)
    s = jnp.einsum('bthd,bnhd->bhtn', q, mem_k).astype(jnp.float32) / math.sqrt(MEM_HEAD_DIM)
    p = jax.nn.softmax(s, axis=-1).astype(mem_v.dtype)
    o = jnp.einsum('bhtn,bnhd->bthd', p, mem_v).reshape(b_sz, t, D_MODEL)
    return o @ w_o


def trunk(x, q_pos0, sb_k_past, sb_v_past, conv_past, hg_past, mem_k, mem_v, params):
    (norm_pre, norm_post, ffn1_gu, ffn1_down, w_in, conv_w, hg_lb, hg_norm, w_out,
     w_mq, w_mo, ffn2_gu, ffn2_down) = params
    b_sz, t, _ = x.shape
    lb_all = hgrn_lower_bounds(hg_lb)
    new_k, new_v, new_conv, new_hg = [], [], [], []
    for l in range(DEPTH):
        h = rmsnorm(x, norm_pre[l, 0])
        x = x + 0.5 * rmsnorm(swiglu(h, ffn1_gu[l], ffn1_down[l]), norm_post[l, 0])
        h = rmsnorm(x, norm_pre[l, 1])
        sb_q, sb_k, sb_v, cv_b, cv_c, cv_h, hg_q, hg_f, hg_i, hg_g = jnp.split(h @ w_in[l], IN_OFFSETS, axis=-1)
        k_l = sb_k.reshape(b_sz, t, SB_HEADS, HEAD_DIM)
        v_l = sb_v.reshape(b_sz, t, SB_HEADS, HEAD_DIM)
        sb_out = stick_breaking_attention(
            sb_q.reshape(b_sz, t, SB_HEADS, HEAD_DIM),
            jnp.concatenate([sb_k_past[l], k_l], axis=1),
            jnp.concatenate([sb_v_past[l], v_l], axis=1), q_pos0).reshape(b_sz, t, SB_WIDTH)
        conv_y, conv_state = causal_short_conv(cv_c * cv_h, conv_past[l], conv_w[l])
        conv_out = cv_b * conv_y
        lb = lb_all[l]
        a = hg_f.astype(jnp.float32)
        f = lb + (1.0 - lb) * jax.nn.sigmoid(a)
        log_f = jnp.log(jnp.maximum(f, F_FLOOR))
        k_hg = (1.0 - lb) * jax.nn.sigmoid(-a)
        o_hg, s_hg = hgrn2_recurrence(
            hg_q.reshape(b_sz, t, HG_HEADS, HEAD_DIM), log_f.reshape(b_sz, t, HG_HEADS, HEAD_DIM),
            k_hg.reshape(b_sz, t, HG_HEADS, HEAD_DIM), hg_i.reshape(b_sz, t, HG_HEADS, HEAD_DIM), hg_past[l])
        o_hg = rmsnorm(o_hg, hg_norm[l].reshape(HG_HEADS, HEAD_DIM)).reshape(b_sz, t, HG_WIDTH) * jax.nn.silu(hg_g)
        mix = jnp.concatenate([sb_out, conv_out, o_hg], axis=-1) @ w_out[l]
        x = x + rmsnorm(mix, norm_post[l, 1])
        h = rmsnorm(x, norm_pre[l, 2])
        x = x + rmsnorm(memory_attention(h, mem_k[l], mem_v[l], w_mq[l], w_mo[l]), norm_post[l, 2])
        h = rmsnorm(x, norm_pre[l, 3])
        x = x + 0.5 * rmsnorm(swiglu(h, ffn2_gu[l], ffn2_down[l]), norm_post[l, 3])
        new_k.append(k_l)
        new_v.append(v_l)
        new_conv.append(conv_state)
        new_hg.append(s_hg.astype(x.dtype))
    return x, jnp.stack(new_k), jnp.stack(new_v), jnp.stack(new_conv), jnp.stack(new_hg)


def setup_inputs(seed: int = 0) -> dict:
    key = jax.random.key(seed)
    ks = jax.random.split(key, 32)
    f32 = jnp.float32

    def nrm(k, shape, scale=1.0):
        return jax.random.normal(k, shape, f32) * scale

    def wt(k, shape, fan_in):
        return jax.random.normal(k, shape, f32) * fan_in ** -0.5

    def gain(k, shape):
        return 1.0 + 0.01 * jax.random.normal(k, shape, f32)

    return {
        "x_prompt": nrm(ks[0], (BATCH, SEQ, D_MODEL)),
        "x_sample": nrm(ks[1], (DEC_BATCH, DEC_SEQ, D_MODEL)),
        "mem_prompt": nrm(ks[2], (BATCH, N_MEM, D_MODEL)),
        "cache_sb_k": nrm(ks[3], (DEPTH, DEC_BATCH, PAST_LEN, SB_HEADS, HEAD_DIM)),
        "cache_sb_v": nrm(ks[4], (DEPTH, DEC_BATCH, PAST_LEN, SB_HEADS, HEAD_DIM)),
        "cache_mem_k": nrm(ks[5], (DEPTH, DEC_BATCH, N_MEM, MEM_HEADS, MEM_HEAD_DIM)),
        "cache_mem_v": nrm(ks[6], (DEPTH, DEC_BATCH, N_MEM, MEM_HEADS, MEM_HEAD_DIM)),
        "state_conv": nrm(ks[7], (DEPTH, DEC_BATCH, CONV_K - 1, CONV_WIDTH)),
        "state_hgrn": nrm(ks[8], (DEPTH, DEC_BATCH, HG_HEADS, HEAD_DIM, HEAD_DIM), 0.5),
        "norm_pre": gain(ks[9], (DEPTH, N_SUBLAYERS, D_MODEL)),
        "norm_post": gain(ks[10], (DEPTH, N_SUBLAYERS, D_MODEL)),
        "ffn1_gu": wt(ks[11], (DEPTH, D_MODEL, 2 * D_FF), D_MODEL),
        "ffn1_down": wt(ks[12], (DEPTH, D_FF, D_MODEL), D_FF),
        "w_in": wt(ks[13], (DEPTH, D_MODEL, IN_WIDTH), D_MODEL),
        "conv_w": wt(ks[14], (DEPTH, CONV_K, CONV_WIDTH), CONV_K),
        "hg_lb": nrm(ks[15], (DEPTH, HG_WIDTH), 0.1),
        "hg_norm": gain(ks[16], (DEPTH, HG_WIDTH)),
        "w_out": wt(ks[17], (DEPTH, MIX_WIDTH, D_MODEL), MIX_WIDTH),
        "mem_norm": gain(ks[18], (DEPTH, D_MODEL)),
        "w_mk": wt(ks[19], (DEPTH, D_MODEL, D_MODEL), D_MODEL),
        "w_mv": wt(ks[20], (DEPTH, D_MODEL, D_MODEL), D_MODEL),
        "w_mq": wt(ks[21], (DEPTH, D_MODEL, D_MODEL), D_MODEL),
        "w_mo": wt(ks[22], (DEPTH, D_MODEL, D_MODEL), D_MODEL),
        "ffn2_gu": wt(ks[23], (DEPTH, D_MODEL, 2 * D_FF), D_MODEL),
        "ffn2_down": wt(ks[24], (DEPTH, D_FF, D_MODEL), D_FF),
    }


def reference(x_prompt, x_sample, mem_prompt, cache_sb_k, cache_sb_v, cache_mem_k, cache_mem_v,
              state_conv, state_hgrn, norm_pre, norm_post, ffn1_gu, ffn1_down, w_in, conv_w, hg_lb,
              hg_norm, w_out, mem_norm, w_mk, w_mv, w_mq, w_mo, ffn2_gu, ffn2_down):
    params = (norm_pre, norm_post, ffn1_gu, ffn1_down, w_in, conv_w, hg_lb, hg_norm, w_out,
              w_mq, w_mo, ffn2_gu, ffn2_down)
    mem_kv_p = [memory_kv(mem_prompt, mem_norm[l], w_mk[l], w_mv[l]) for l in range(DEPTH)]
    mem_k_p = jnp.stack([kv[0] for kv in mem_kv_p])
    mem_v_p = jnp.stack([kv[1] for kv in mem_kv_p])
    b_p = x_prompt.shape[0]
    dt = x_prompt.dtype
    empty_kv = jnp.zeros((DEPTH, b_p, 0, SB_HEADS, HEAD_DIM), dt)
    conv0 = jnp.zeros((DEPTH, b_p, CONV_K - 1, CONV_WIDTH), dt)
    hg0 = jnp.zeros((DEPTH, b_p, HG_HEADS, HEAD_DIM, HEAD_DIM), jnp.float32)
    y_prompt, sb_k_p, sb_v_p, conv_p, hg_p = trunk(
        x_prompt, 0, empty_kv, empty_kv, conv0, hg0, mem_k_p, mem_v_p, params)
    y_sample, sb_k_s, sb_v_s, conv_s, hg_s = trunk(
        x_sample, cache_sb_k.shape[2], cache_sb_k, cache_sb_v, state_conv, state_hgrn,
        cache_mem_k, cache_mem_v, params)
    return (y_prompt, y_sample, sb_k_p, sb_v_p, conv_p, hg_p, mem_k_p, mem_v_p,
            sb_k_s, sb_v_s, conv_s, hg_s)
```

```python
import functools
import math

import numpy as np
import jax
import jax.numpy as jnp
from jax import lax
from jax.experimental import pallas as pl
from jax.experimental.pallas import tpu as pltpu

F32 = jnp.float32
BF16 = jnp.bfloat16

EPS = 1e-6
F_FLOOR = 1e-30
HEAD_DIM = 64
SB_HEADS = 8
SB_WIDTH = SB_HEADS * HEAD_DIM
CONV_WIDTH = 256
CONV_K = 3
HG_HEADS = 4
HG_WIDTH = HG_HEADS * HEAD_DIM
MEM_HEADS = 4
CHUNK = 64
LANES = 128
SUBLANES = 8
VMEM_LIMIT = 56 * 1024 * 1024


def _params(*sem):
    return pltpu.CompilerParams(dimension_semantics=sem, vmem_limit_bytes=VMEM_LIMIT)


def _rms(x, g):
    ms = jnp.mean(x * x, axis=-1, keepdims=True)
    return x * lax.rsqrt(ms + EPS) * g


def _split_bf16(x):
    hi = x.astype(BF16)
    lo = (x - hi.astype(F32)).astype(BF16)
    return hi, lo


def _row_tile(n, want):
    t = min(n, want)
    while n % t:
        t //= 2
    return t


def _ffn_kernel(x_ref, gpre_ref, gpost_ref, wg_ref, wu_ref, wd_ref, o_ref, h_scr, acc_scr):
    j = pl.program_id(1)

    @pl.when(j == 0)
    def _():
        h_scr[...] = _rms(x_ref[...], gpre_ref[...]).astype(BF16)
        acc_scr[...] = jnp.zeros_like(acc_scr)

    h = h_scr[...]
    gate = jnp.dot(h, wg_ref[...], preferred_element_type=F32)
    up = jnp.dot(h, wu_ref[...], preferred_element_type=F32)
    act = (gate * jax.nn.sigmoid(gate) * up).astype(BF16)
    acc_scr[...] += jnp.dot(act, wd_ref[...], preferred_element_type=F32)

    @pl.when(j == pl.num_programs(1) - 1)
    def _():
        o_ref[...] = x_ref[...] + 0.5 * _rms(acc_scr[...], gpost_ref[...])


def _ffn(x, gpre, gpost, w_gu, w_down, *, tm, tf):
    n, d = x.shape
    d_ff = w_down.shape[0]
    nj = d_ff // tf
    return pl.pallas_call(
        _ffn_kernel,
        grid=(n // tm, nj),
        in_specs=[
            pl.BlockSpec((tm, d), lambda i, j: (i, 0)),
            pl.BlockSpec((1, d), lambda i, j: (0, 0)),
            pl.BlockSpec((1, d), lambda i, j: (0, 0)),
            pl.BlockSpec((d, tf), lambda i, j: (0, j)),
            pl.BlockSpec((d, tf), lambda i, j: (0, j + nj)),
            pl.BlockSpec((tf, d), lambda i, j: (j, 0)),
        ],
        out_specs=pl.BlockSpec((tm, d), lambda i, j: (i, 0)),
        out_shape=jax.ShapeDtypeStruct((n, d), F32),
        scratch_shapes=[pltpu.VMEM((tm, d), BF16), pltpu.VMEM((tm, d), F32)],
        compiler_params=_params("parallel", "arbitrary"),
        name="ffn",
    )(x, gpre, gpost, w_gu, w_gu, w_down)


def _norm_proj_kernel(*refs, n_w, segments):
    x_ref, g_ref = refs[0], refs[1]
    w_refs = refs[2:2 + n_w]
    o_refs = refs[2 + n_w:]
    h = _rms(x_ref[...], g_ref[...]).astype(BF16)
    for o_ref, (wi, off, width) in zip(o_refs, segments):
        o_ref[...] = jnp.dot(h, w_refs[wi][:, off:off + width],
                             preferred_element_type=F32).astype(o_ref.dtype)


def _norm_proj(x, g, weights, segments, out_dtypes, *, tm):
    n, d = x.shape
    in_specs = [pl.BlockSpec((tm, d), lambda i: (i, 0)), pl.BlockSpec((1, d), lambda i: (0, 0))]
    in_specs += [pl.BlockSpec(w.shape, lambda i: (0, 0)) for w in weights]
    out_specs = [pl.BlockSpec((tm, width), lambda i: (i, 0)) for (_, _, width) in segments]
    out_shape = [jax.ShapeDtypeStruct((n, width), dt) for (_, _, width), dt in zip(segments, out_dtypes)]
    return pl.pallas_call(
        functools.partial(_norm_proj_kernel, n_w=len(weights), segments=tuple(segments)),
        grid=(n // tm,),
        in_specs=in_specs,
        out_specs=out_specs,
        out_shape=out_shape,
        compiler_params=_params("parallel"),
        name="norm_proj",
    )(x, g, *weights)


def _sb_kernel(q_ref, k_ref, v_ref, tri_ref, o_ref, acc_scr, carry_scr, *, blk, diag0, scale):
    jd = diag0 + pl.program_id(2)
    lane = lax.broadcasted_iota(jnp.int32, (1, LANES), 1)
    even = lane < HEAD_DIM
    q2 = q_ref[...]
    zero = jnp.zeros_like(q2)
    q_heads = (jnp.where(even, q2, zero), jnp.where(even, zero, q2))
    row = lax.broadcasted_iota(jnp.int32, (blk, blk), 0)
    col = lax.broadcasted_iota(jnp.int32, (blk, blk), 1)
    strict = col < row

    acc_scr[...] = jnp.zeros_like(acc_scr)
    carry_scr[...] = jnp.zeros_like(carry_scr)

    def block(j, masked):
        start = pl.multiple_of(j * blk, blk)
        kb = k_ref[pl.ds(start, blk), :].astype(BF16)
        vb = v_ref[pl.ds(start, blk), :].astype(BF16)
        vzero = jnp.zeros_like(vb)
        v_heads = (jnp.where(even, vb, vzero), jnp.where(even, vzero, vb))
        out = acc_scr[...]
        for hh in range(2):
            z = lax.dot_general(q_heads[hh], kb, (((1,), (1,)), ((), ())),
                                preferred_element_type=F32) * scale
            sp = jnp.log(1.0 + jnp.exp(-jnp.abs(z)))
            log_beta = jnp.minimum(z, 0.0) - sp
            log_keep = jnp.minimum(-z, 0.0) - sp
            if masked:
                log_keep = jnp.where(strict, log_keep, 0.0)
            hi, lo = _split_bf16(log_keep)
            sums = jnp.dot(jnp.concatenate([hi, lo], axis=1), tri_ref[...],
                           preferred_element_type=F32)
            carry = carry_scr[hh]
            arg = log_beta + sums[:, :blk] + carry
            if masked:
                w = jnp.where(strict, jnp.exp(jnp.where(strict, arg, 0.0)), 0.0)
            else:
                w = jnp.exp(arg)
            out = out + jnp.dot(w.astype(BF16), v_heads[hh], preferred_element_type=F32)
            carry_scr[hh] = carry + sums[:, blk:]
        acc_scr[...] = out

    block(jd, True)

    def body(it, c):
        block(jd - 1 - it, False)
        return c

    lax.fori_loop(0, jd, body, 0)
    o_ref[...] = acc_scr[...].astype(o_ref.dtype)


def _sb_tri(blk):
    s_from = np.arange(blk)[:, None]
    s_to = np.arange(blk)[None, :]
    half = np.concatenate([(s_from > s_to), np.ones((blk, blk), bool)], axis=1)
    return jnp.asarray(np.concatenate([half, half], axis=0), dtype=BF16)


def _sb_attention(q, k, v, *, blk, q_pos0):
    b_sz, tq, _ = q.shape
    tk = k.shape[1]
    assert tq % blk == 0 and q_pos0 % blk == 0 and tk == q_pos0 + tq
    n_pairs = SB_WIDTH // LANES
    kern = functools.partial(_sb_kernel, blk=blk, diag0=q_pos0 // blk, scale=1.0 / math.sqrt(HEAD_DIM))
    return pl.pallas_call(
        kern,
        grid=(b_sz, n_pairs, tq // blk),
        in_specs=[
            pl.BlockSpec((None, blk, LANES), lambda b, p, i: (b, i, p)),
            pl.BlockSpec((None, tk, LANES), lambda b, p, i: (b, 0, p)),
            pl.BlockSpec((None, tk, LANES), lambda b, p, i: (b, 0, p)),
            pl.BlockSpec((2 * blk, 2 * blk), lambda b, p, i: (0, 0)),
        ],
        out_specs=pl.BlockSpec((None, blk, LANES), lambda b, p, i: (b, i, p)),
        out_shape=jax.ShapeDtypeStruct((b_sz, tq, SB_WIDTH), BF16),
        scratch_shapes=[pltpu.VMEM((blk, LANES), F32), pltpu.VMEM((2, blk, blk), F32)],
        compiler_params=_params("parallel", "parallel", "arbitrary"),
        name="sb_attention",
    )(q, k, v, _sb_tri(blk))


def _tri_rows():
    offs, sizes = [], []
    off = 0
    for t in range(CHUNK):
        n = SUBLANES * (t // SUBLANES + 1)
        offs.append(off)
        sizes.append(n)
        off += n
    return offs, sizes, off


_TRI_OFFS, _TRI_SIZES, _TRI_ROWS = _tri_rows()


def _mixer_consts():
    head = np.arange(HG_WIDTH) // HEAD_DIM
    same_head = (head[:, None] == head[None, :])
    sel = np.zeros((CHUNK, _TRI_ROWS), bool)
    for t in range(CHUNK):
        sel[t, _TRI_OFFS[t]:_TRI_OFFS[t] + _TRI_SIZES[t]] = True
    incl = np.arange(CHUNK)[:, None] >= np.arange(CHUNK)[None, :]
    cum = np.concatenate([incl, incl], axis=1)
    return (jnp.asarray(same_head, dtype=BF16), jnp.asarray(same_head, dtype=F32),
            jnp.asarray(sel, dtype=BF16), jnp.asarray(cum, dtype=BF16))


def _mixer_kernel(cv_ref, hg_ref, cpast_ref, cw_ref, lbl_ref, gn_ref, st0_ref,
                  bd_ref, bdf_ref, sel_ref, cum_ref,
                  conv_o_ref, hg_o_ref, cstate_ref, st_ref,
                  u_scr, b_scr, k_scr, a_scr, *, layer, tt):
    ti = pl.program_id(1)
    last = pl.num_programs(1) - 1
    pad = SUBLANES

    @pl.when(ti == 0)
    def _():
        u_scr[pad - 2:pad, :] = cpast_ref[...]
        st_ref[...] = st0_ref[...]

    cb = cv_ref[:, 0:CONV_WIDTH]
    u_scr[pad:pad + tt, :] = cv_ref[:, CONV_WIDTH:2 * CONV_WIDTH] * cv_ref[:, 2 * CONV_WIDTH:3 * CONV_WIDTH]
    y = u_scr[pad - 2:pad - 2 + tt, :] * cw_ref[0:1, :]
    y = y + u_scr[pad - 1:pad - 1 + tt, :] * cw_ref[1:2, :]
    y = y + u_scr[pad:pad + tt, :] * cw_ref[2:3, :]
    conv_o_ref[...] = (cb * y).astype(conv_o_ref.dtype)
    tail = u_scr[pad + tt - 2:pad + tt, :]
    u_scr[pad - 2:pad, :] = tail

    @pl.when(ti == last)
    def _():
        cstate_ref[...] = tail

    logits = lbl_ref[...]
    mx = jnp.max(logits, axis=0, keepdims=True)
    ex = jnp.exp(logits - mx)
    den = jnp.sum(ex, axis=0, keepdims=True)
    lb = jnp.zeros_like(den)
    for i in range(1, layer + 1):
        lb = lb + ex[i:i + 1, :] / den
    one_m_lb = 1.0 - lb

    gn = gn_ref[...]
    sub_row = lax.broadcasted_iota(jnp.int32, (SUBLANES, HG_WIDTH), 0)

    def chunk(c, carry):
        r0 = pl.multiple_of(c * CHUNK, CHUNK)
        rows = pl.ds(r0, CHUNK)
        qh = hg_ref[rows, 0:HG_WIDTH]
        a = hg_ref[rows, HG_WIDTH:2 * HG_WIDTH]
        vi = hg_ref[rows, 2 * HG_WIDTH:3 * HG_WIDTH]
        gg = hg_ref[rows, 3 * HG_WIDTH:4 * HG_WIDTH]

        e = jnp.exp(-jnp.abs(a))
        r = 1.0 / (1.0 + e)
        pos = a >= 0.0
        sig = jnp.where(pos, r, e * r)
        nsig = jnp.where(pos, e * r, r)
        f = lb + one_m_lb * sig
        g = jnp.log(jnp.maximum(f, F_FLOOR))
        kk = one_m_lb * nsig
        g_hi, g_lo = _split_bf16(g)
        b = jnp.dot(cum_ref[...], jnp.concatenate([g_hi, g_lo], axis=0),
                    preferred_element_type=F32)
        b_scr[...] = b
        k_scr[...] = kk

        for t in range(CHUNK):
            n = _TRI_SIZES[t]
            qt = hg_ref[pl.ds(r0 + t, 1), 0:HG_WIDTH]
            bt = b_scr[t:t + 1, :]
            diff = bt - b_scr[0:n, :]
            a_full = qt * jnp.exp(diff[:n - SUBLANES]) * k_scr[0:n - SUBLANES, :] if n > SUBLANES else None
            ok = sub_row <= (t % SUBLANES)
            d_last = jnp.where(ok, diff[n - SUBLANES:], 0.0)
            a_last = jnp.where(ok, qt * jnp.exp(d_last) * k_scr[n - SUBLANES:n, :], 0.0)
            if a_full is not None:
                a_scr[_TRI_OFFS[t]:_TRI_OFFS[t] + n - SUBLANES, :] = a_full
            a_scr[_TRI_OFFS[t] + n - SUBLANES:_TRI_OFFS[t] + n, :] = a_last
        scores = jnp.dot(a_scr[...].astype(BF16), bd_ref[...], preferred_element_type=F32)
        p = jnp.concatenate(
            [scores[_TRI_OFFS[t]:_TRI_OFFS[t] + _TRI_SIZES[t]] * vi[0:_TRI_SIZES[t]] for t in range(CHUNK)],
            axis=0)
        intra = jnp.dot(sel_ref[...], p.astype(BF16), preferred_element_type=F32)

        st = st_ref[...]
        inter = lax.dot_general((qh * jnp.exp(b)).astype(BF16), st.astype(BF16),
                                (((1,), (1,)), ((), ())), preferred_element_type=F32)
        b_last = b[CHUNK - 1:CHUNK, :]
        kd = kk * jnp.exp(b_last - b)
        upd = lax.dot_general(vi.astype(BF16), kd.astype(BF16), (((0,), (0,)), ((), ())),
                              preferred_element_type=F32)
        st_ref[...] = st * jnp.exp(b_last) + upd * bdf_ref[...]

        o = inter + intra
        o_hi, o_lo = _split_bf16(o * o)
        ms = (jnp.dot(o_hi, bd_ref[...], preferred_element_type=F32)
              + jnp.dot(o_lo, bd_ref[...], preferred_element_type=F32)) * (1.0 / HEAD_DIM)
        yo = o * lax.rsqrt(ms + EPS) * gn * (gg * jax.nn.sigmoid(gg))
        hg_o_ref[rows, :] = yo.astype(hg_o_ref.dtype)
        return carry

    lax.fori_loop(0, tt // CHUNK, chunk, 0)


def _mixer(cv, hg, conv_past, conv_w, lb_logits, hg_norm, st0, consts, *, layer, tt):
    b_sz, t_len, _ = cv.shape
    assert t_len % tt == 0 and tt % CHUNK == 0
    bd, bdf, sel, cum = consts
    depth = lb_logits.shape[0]
    const2 = lambda b, t: (0, 0)
    return pl.pallas_call(
        functools.partial(_mixer_kernel, layer=layer, tt=tt),
        grid=(b_sz, t_len // tt),
        in_specs=[
            pl.BlockSpec((None, tt, 3 * CONV_WIDTH), lambda b, t: (b, t, 0)),
            pl.BlockSpec((None, tt, 4 * HG_WIDTH), lambda b, t: (b, t, 0)),
            pl.BlockSpec((None, CONV_K - 1, CONV_WIDTH), lambda b, t: (b, 0, 0)),
            pl.BlockSpec((CONV_K, CONV_WIDTH), const2),
            pl.BlockSpec((depth, HG_WIDTH), const2),
            pl.BlockSpec((1, HG_WIDTH), const2),
            pl.BlockSpec((None, HG_WIDTH, HG_WIDTH), lambda b, t: (b, 0, 0)),
            pl.BlockSpec(bd.shape, const2),
            pl.BlockSpec(bdf.shape, const2),
            pl.BlockSpec(sel.shape, const2),
            pl.BlockSpec(cum.shape, const2),
        ],
        out_specs=[
            pl.BlockSpec((None, tt, CONV_WIDTH), lambda b, t: (b, t, 0)),
            pl.BlockSpec((None, tt, HG_WIDTH), lambda b, t: (b, t, 0)),
            pl.BlockSpec((None, CONV_K - 1, CONV_WIDTH), lambda b, t: (b, 0, 0)),
            pl.BlockSpec((None, HG_WIDTH, HG_WIDTH), lambda b, t: (b, 0, 0)),
        ],
        out_shape=[
            jax.ShapeDtypeStruct((b_sz, t_len, CONV_WIDTH), BF16),
            jax.ShapeDtypeStruct((b_sz, t_len, HG_WIDTH), BF16),
            jax.ShapeDtypeStruct((b_sz, CONV_K - 1, CONV_WIDTH), F32),
            jax.ShapeDtypeStruct((b_sz, HG_WIDTH, HG_WIDTH), F32),
        ],
        scratch_shapes=[
            pltpu.VMEM((tt + SUBLANES, CONV_WIDTH), F32),
            pltpu.VMEM((CHUNK, HG_WIDTH), F32),
            pltpu.VMEM((CHUNK, HG_WIDTH), F32),
            pltpu.VMEM((_TRI_ROWS, HG_WIDTH), F32),
        ],
        compiler_params=_params("parallel", "arbitrary"),
        name="mixer",
    )(cv, hg, conv_past, conv_w, lb_logits, hg_norm, st0, bd, bdf, sel, cum)


def _out_mem_kernel(x_ref, sb_ref, cv_ref, hg_ref, mk_ref, mv_ref, wo_ref, wq_ref, wmo_ref,
                    gpost1_ref, gpre2_ref, gpost2_ref, o_ref, *, scale):
    mix = jnp.dot(sb_ref[...], wo_ref[0:SB_WIDTH, :], preferred_element_type=F32)
    mix = mix + jnp.dot(cv_ref[...], wo_ref[SB_WIDTH:SB_WIDTH + CONV_WIDTH, :], preferred_element_type=F32)
    mix = mix + jnp.dot(hg_ref[...], wo_ref[SB_WIDTH + CONV_WIDTH:, :], preferred_element_type=F32)
    x = x_ref[...] + _rms(mix, gpost1_ref[...])

    h = _rms(x, gpre2_ref[...]).astype(BF16)
    q = jnp.dot(h, wq_ref[...], preferred_element_type=F32).astype(BF16)
    d = x.shape[1]
    dh = d // MEM_HEADS
    heads = []
    for hd in range(MEM_HEADS):
        cols = slice(hd * dh, (hd + 1) * dh)
        s = lax.dot_general(q[:, cols], mk_ref[:, cols].astype(BF16), (((1,), (1,)), ((), ())),
                            preferred_element_type=F32) * scale
        s = s - jnp.max(s, axis=-1, keepdims=True)
        p = jnp.exp(s)
        p = p / jnp.sum(p, axis=-1, keepdims=True)
        heads.append(jnp.dot(p.astype(BF16), mv_ref[:, cols].astype(BF16), preferred_element_type=F32))
    att = jnp.concatenate(heads, axis=1).astype(BF16)
    out = jnp.dot(att, wmo_ref[...], preferred_element_type=F32)
    o_ref[...] = x + _rms(out, gpost2_ref[...])


def _out_mem(x, sb, cv, hg, mem_k, mem_v, w_out, w_mq, w_mo, gpost1, gpre2, gpost2, *, tm):
    b_sz, t_len, d = x.shape
    n_mem = mem_k.shape[1]
    tok = lambda width: pl.BlockSpec((None, tm, width), lambda b, t: (b, t, 0))
    per_b = pl.BlockSpec((None, n_mem, d), lambda b, t: (b, 0, 0))
    full = lambda shape: pl.BlockSpec(shape, lambda b, t: (0, 0))
    return pl.pallas_call(
        functools.partial(_out_mem_kernel, scale=1.0 / math.sqrt(d // MEM_HEADS)),
        grid=(b_sz, t_len // tm),
        in_specs=[tok(d), tok(SB_WIDTH), tok(CONV_WIDTH), tok(HG_WIDTH), per_b, per_b,
                  full(w_out.shape), full(w_mq.shape), full(w_mo.shape),
                  full((1, d)), full((1, d)), full((1, d))],
        out_specs=tok(d),
        out_shape=jax.ShapeDtypeStruct((b_sz, t_len, d), F32),
        compiler_params=_params("parallel", "parallel"),
        name="out_mem",
    )(x, sb, cv, hg, mem_k, mem_v, w_out, w_mq, w_mo, gpost1, gpre2, gpost2)


IN_SEGMENTS = (
    (0, 0, SB_WIDTH),
    (0, SB_WIDTH, SB_WIDTH),
    (0, 2 * SB_WIDTH, SB_WIDTH),
    (0, 3 * SB_WIDTH, 3 * CONV_WIDTH),
    (0, 3 * SB_WIDTH + 3 * CONV_WIDTH, 4 * HG_WIDTH),
)
IN_DTYPES = (BF16, F32, F32, F32, F32)


def _state_to_kernel(s):
    b_sz = s.shape[0]
    eye = jnp.eye(HG_HEADS, dtype=s.dtype)
    return jnp.einsum('bhkv,hg->bhvgk', s, eye).reshape(b_sz, HG_WIDTH, HG_WIDTH)


def _state_from_kernel(st):
    b_sz = st.shape[0]
    st5 = st.reshape(b_sz, HG_HEADS, HEAD_DIM, HG_HEADS, HEAD_DIM)
    diag = jnp.stack([st5[:, h, :, h, :] for h in range(HG_HEADS)], axis=1)
    return jnp.swapaxes(diag, -1, -2)


def _trunk(x, q_pos0, sb_k_past, sb_v_past, conv_past, hg_past, mem_k, mem_v, w, consts):
    b_sz, t_len, d = x.shape
    depth = w["w_in"].shape[0]
    n = b_sz * t_len
    tm_ffn = _row_tile(n, 1024)
    tm_proj = _row_tile(n, 512)
    tm_tok = _row_tile(t_len, 512)
    blk = 128 if (t_len % 128 == 0 and q_pos0 % 128 == 0) else CHUNK
    d_ff = w["ffn1_down"].shape[1]
    tf = 256 if d_ff % 256 == 0 else d_ff
    row = lambda a, l, s: a[l, s][None, :]

    new_k, new_v, new_conv, new_hg = [], [], [], []
    for l in range(depth):
        x2 = _ffn(x.reshape(n, d), row(w["norm_pre"], l, 0), row(w["norm_post"], l, 0),
                  w["ffn1_gu"][l], w["ffn1_down"][l], tm=tm_ffn, tf=tf)
        sb_q, sb_k, sb_v, cv, hg = _norm_proj(x2, row(w["norm_pre"], l, 1), [w["w_in"][l]],
                                              IN_SEGMENTS, IN_DTYPES, tm=tm_proj)
        k_l = sb_k.reshape(b_sz, t_len, SB_WIDTH)
        v_l = sb_v.reshape(b_sz, t_len, SB_WIDTH)
        if q_pos0:
            k_all = jnp.concatenate([sb_k_past[l].reshape(b_sz, q_pos0, SB_WIDTH), k_l], axis=1)
            v_all = jnp.concatenate([sb_v_past[l].reshape(b_sz, q_pos0, SB_WIDTH), v_l], axis=1)
        else:
            k_all, v_all = k_l, v_l
        sb_out = _sb_attention(sb_q.reshape(b_sz, t_len, SB_WIDTH), k_all, v_all, blk=blk, q_pos0=q_pos0)
        conv_out, hg_out, conv_state, st = _mixer(
            cv.reshape(b_sz, t_len, -1), hg.reshape(b_sz, t_len, -1), conv_past[l], w["conv_w"][l],
            w["hg_lb"], w["hg_norm"][l][None, :], _state_to_kernel(hg_past[l]), consts,
            layer=l, tt=tm_tok)
        x3 = _out_mem(x2.reshape(b_sz, t_len, d), sb_out, conv_out, hg_out,
                      mem_k[l].reshape(b_sz, -1, d), mem_v[l].reshape(b_sz, -1, d),
                      w["w_out"][l], w["w_mq"][l], w["w_mo"][l],
                      row(w["norm_post"], l, 1), row(w["norm_pre"], l, 2), row(w["norm_post"], l, 2),
                      tm=tm_tok)
        x = _ffn(x3.reshape(n, d), row(w["norm_pre"], l, 3), row(w["norm_post"], l, 3),
                 w["ffn2_gu"][l], w["ffn2_down"][l], tm=tm_ffn, tf=tf).reshape(b_sz, t_len, d)
        new_k.append(k_l.reshape(b_sz, t_len, SB_HEADS, HEAD_DIM))
        new_v.append(v_l.reshape(b_sz, t_len, SB_HEADS, HEAD_DIM))
        new_conv.append(conv_state)
        new_hg.append(_state_from_kernel(st))
    return x, jnp.stack(new_k), jnp.stack(new_v), jnp.stack(new_conv), jnp.stack(new_hg)


def kernel(x_prompt, x_sample, mem_prompt, cache_sb_k, cache_sb_v, cache_mem_k, cache_mem_v,
           state_conv, state_hgrn, norm_pre, norm_post, ffn1_gu, ffn1_down, w_in, conv_w, hg_lb,
           hg_norm, w_out, mem_norm, w_mk, w_mv, w_mq, w_mo, ffn2_gu, ffn2_down):
    depth = w_in.shape[0]
    b_p, n_mem, d = mem_prompt.shape
    bf = lambda a: a.astype(BF16)
    w = dict(norm_pre=norm_pre, norm_post=norm_post, ffn1_gu=bf(ffn1_gu), ffn1_down=bf(ffn1_down),
             w_in=bf(w_in), conv_w=conv_w, hg_lb=hg_lb, hg_norm=hg_norm, w_out=bf(w_out),
             w_mq=bf(w_mq), w_mo=bf(w_mo), ffn2_gu=bf(ffn2_gu), ffn2_down=bf(ffn2_down))
    consts = _mixer_consts()

    mem_flat = mem_prompt.reshape(b_p * n_mem, d)
    w_mk_b, w_mv_b = bf(w_mk), bf(w_mv)
    mem_k_l, mem_v_l = [], []
    for l in range(depth):
        mk, mv = _norm_proj(mem_flat, mem_norm[l][None, :], [w_mk_b[l], w_mv_b[l]],
                            ((0, 0, d), (1, 0, d)), (F32, F32), tm=_row_tile(b_p * n_mem, 512))
        mem_k_l.append(mk.reshape(b_p, n_mem, MEM_HEADS, d // MEM_HEADS))
        mem_v_l.append(mv.reshape(b_p, n_mem, MEM_HEADS, d // MEM_HEADS))
    mem_k_p = jnp.stack(mem_k_l)
    mem_v_p = jnp.stack(mem_v_l)

    dt = x_prompt.dtype
    conv0 = jnp.zeros((depth, b_p, CONV_K - 1, CONV_WIDTH), dt)
    hg0 = jnp.zeros((depth, b_p, HG_HEADS, HEAD_DIM, HEAD_DIM), F32)
    y_p, sb_k_p, sb_v_p, conv_p, hg_p = _trunk(
        x_prompt, 0, None, None, conv0, hg0, mem_k_p, mem_v_p, w, consts)
    y_s, sb_k_s, sb_v_s, conv_s, hg_s = _trunk(
        x_sample, cache_sb_k.shape[2], cache_sb_k, cache_sb_v, state_conv, state_hgrn,
        cache_mem_k, cache_mem_v, w, consts)
    return (y_p, y_s, sb_k_p, sb_v_p, conv_p, hg_p, mem_k_p, mem_v_p, sb_k_s, sb_v_s, conv_s, hg_s)
```

```python
import functools
import math

import numpy as np
import jax
import jax.numpy as jnp
from jax import lax
from jax.experimental import pallas as pl
from jax.experimental.pallas import tpu as pltpu

F32 = jnp.float32
BF16 = jnp.bfloat16

EPS = 1e-6
F_FLOOR = 1e-30
HEAD_DIM = 64
SB_HEADS = 8
SB_WIDTH = SB_HEADS * HEAD_DIM
CONV_WIDTH = 256
CONV_K = 3
HG_HEADS = 4
HG_WIDTH = HG_HEADS * HEAD_DIM
MEM_HEADS = 4
CHUNK = 64
FFN_SLAB = 256
SB_PAIRS_PER_STEP = 2
LANES = 128
SUBLANES = 8
VMEM_LIMIT = 56 * 1024 * 1024


def _params(*sem):
    return pltpu.CompilerParams(dimension_semantics=sem, vmem_limit_bytes=VMEM_LIMIT)


def _rms(x, g):
    ms = jnp.mean(x * x, axis=-1, keepdims=True)
    return x * lax.rsqrt(ms + EPS) * g


def _split_bf16(x):
    hi = x.astype(BF16)
    lo = (x - hi.astype(F32)).astype(BF16)
    return hi, lo


def _row_tile(n, want):
    t = min(n, want)
    while n % t:
        t //= 2
    return t


def _ffn_kernel(x_ref, gpre_ref, gpost_ref, wgu_ref, wd_ref, o_ref, h_scr, acc_scr):
    h_scr[...] = _rms(x_ref[...], gpre_ref[...]).astype(BF16)
    acc_scr[...] = jnp.zeros_like(acc_scr)

    def slab(j, c):
        h = h_scr[...]
        gate = jnp.dot(h, wgu_ref[0, j], preferred_element_type=F32)
        up = jnp.dot(h, wgu_ref[1, j], preferred_element_type=F32)
        act = (gate * jax.nn.sigmoid(gate) * up).astype(BF16)
        acc_scr[...] += jnp.dot(act, wd_ref[j], preferred_element_type=F32)
        return c

    lax.fori_loop(0, wd_ref.shape[0], slab, 0)
    o_ref[...] = x_ref[...] + 0.5 * _rms(acc_scr[...], gpost_ref[...])


def _ffn_weights(w_gu, w_down, tf):
    d, two_ff = w_gu.shape
    nj = two_ff // 2 // tf
    return (w_gu.astype(BF16).reshape(d, 2, nj, tf).transpose(1, 2, 0, 3),
            w_down.astype(BF16).reshape(nj, tf, d))


def _ffn(x, gpre, gpost, w_gu, w_down, *, tm):
    n, d = x.shape
    resident = lambda a: pl.BlockSpec(a.shape, lambda i: (0,) * a.ndim, pipeline_mode=pl.Buffered(1))
    return pl.pallas_call(
        _ffn_kernel,
        grid=(n // tm,),
        in_specs=[
            pl.BlockSpec((tm, d), lambda i: (i, 0)),
            pl.BlockSpec((1, d), lambda i: (0, 0)),
            pl.BlockSpec((1, d), lambda i: (0, 0)),
            resident(w_gu),
            resident(w_down),
        ],
        out_specs=pl.BlockSpec((tm, d), lambda i: (i, 0)),
        out_shape=jax.ShapeDtypeStruct((n, d), F32),
        scratch_shapes=[pltpu.VMEM((tm, d), BF16), pltpu.VMEM((tm, d), F32)],
        compiler_params=_params("parallel"),
        name="ffn",
    )(x, gpre, gpost, w_gu, w_down)


def _norm_proj_kernel(*refs, n_w, segments):
    x_ref, g_ref = refs[0], refs[1]
    w_refs = refs[2:2 + n_w]
    o_refs = refs[2 + n_w:]
    h = _rms(x_ref[...], g_ref[...]).astype(BF16)
    o_iter = iter(o_refs)
    for wi, off, width, dtypes in segments:
        res = jnp.dot(h, w_refs[wi][:, off:off + width], preferred_element_type=F32)
        for _ in dtypes:
            o_ref = next(o_iter)
            o_ref[...] = res.astype(o_ref.dtype)


def _norm_proj(x, g, weights, segments, *, tm):
    n, d = x.shape
    in_specs = [pl.BlockSpec((tm, d), lambda i: (i, 0)), pl.BlockSpec((1, d), lambda i: (0, 0))]
    in_specs += [pl.BlockSpec(w.shape, lambda i: (0, 0)) for w in weights]
    out_specs = [pl.BlockSpec((tm, width), lambda i: (i, 0)) for (_, _, width, dts) in segments for _ in dts]
    out_shape = [jax.ShapeDtypeStruct((n, width), dt) for (_, _, width, dts) in segments for dt in dts]
    return pl.pallas_call(
        functools.partial(_norm_proj_kernel, n_w=len(weights), segments=tuple(segments)),
        grid=(n // tm,),
        in_specs=in_specs,
        out_specs=out_specs,
        out_shape=out_shape,
        compiler_params=_params("parallel"),
        name="norm_proj",
    )(x, g, *weights)


def _sb_kernel(q_ref, k_ref, v_ref, tri_ref, o_ref, acc_scr, carry_scr, z_scr, *, bq, n_sub, n_pair, q_pos0, scale):
    width = n_sub * LANES
    pos0 = q_pos0 + pl.program_id(2) * bq
    sd = (pos0 + bq - 1) // width
    off = pos0 - sd * width
    lane = lax.broadcasted_iota(jnp.int32, (1, LANES), 1)
    even = lane < HEAD_DIM
    heads = [(pp, hh) for pp in range(n_pair) for hh in range(2)]
    pair_cols = lambda pp: slice(pp * LANES, (pp + 1) * LANES)
    log2_scale = scale * math.log2(math.e)

    def split_pair(x):
        zero = jnp.zeros_like(x)
        return jnp.where(even, x, zero), jnp.where(even, zero, x)

    q_heads = [split_pair(q_ref[:, pair_cols(pp)]) for pp in range(n_pair)]
    acc_scr[...] = jnp.zeros_like(acc_scr)
    carry_scr[...] = jnp.zeros_like(carry_scr)

    def logits(s, nc):
        start = pl.multiple_of(s * width, width)
        out = []
        for pp, hh in heads:
            kb = k_ref[pl.ds(start, nc * LANES), pair_cols(pp)]
            out.append(lax.dot_general(q_heads[pp][hh], kb, (((1,), (1,)), ((), ())),
                                       preferred_element_type=F32) * log2_scale)
        return out

    def weigh(s, zs, nc, last_off):
        start = pl.multiple_of(s * width, width)
        if last_off is not None:
            row = lax.broadcasted_iota(jnp.int32, (bq, LANES), 0)
            col = lax.broadcasted_iota(jnp.int32, (bq, LANES), 1)
            strict = col < row + last_off
        for pp in range(n_pair):
            v_heads = split_pair(v_ref[pl.ds(start, nc * LANES), pair_cols(pp)])
            out = acc_scr[:, pair_cols(pp)]
            for hh in range(2):
                h = 2 * pp + hh
                z = zs[h]
                mn = jnp.minimum(z, 0.0)
                nz = mn - z
                sp = jnp.log2(1.0 + jnp.exp2(mn + nz))
                log_beta = mn - sp
                log_keep = nz - sp
                carry = carry_scr[h]
                ws = [None] * nc
                for c in reversed(range(nc)):
                    cols = slice(c * LANES, (c + 1) * LANES)
                    masked = last_off is not None and c == nc - 1
                    lk = jnp.where(strict, log_keep[:, cols], 0.0) if masked else log_keep[:, cols]
                    hi, lo = _split_bf16(lk)
                    sums = jnp.dot(jnp.concatenate([hi, lo], axis=1), tri_ref[...], preferred_element_type=F32)
                    arg = log_beta[:, cols] + sums[:, :LANES] + carry
                    if masked:
                        w = jnp.where(strict, jnp.exp2(jnp.where(strict, arg, 0.0)), 0.0)
                    else:
                        w = jnp.exp2(arg)
                    ws[c] = w.astype(BF16)
                    carry = carry + sums[:, LANES:]
                carry_scr[h] = carry
                w_all = ws[0] if nc == 1 else jnp.concatenate(ws, axis=1)
                out = out + jnp.dot(w_all, v_heads[hh], preferred_element_type=F32)
            acc_scr[:, pair_cols(pp)] = out

    def prefetch_logits(s):
        zs = logits(jnp.maximum(s, 0), n_sub)
        for h in range(len(heads)):
            z_scr[h] = zs[h]

    def diagonal(nc):
        def run():
            weigh(sd, logits(sd, nc), nc, off - (nc - 1) * LANES)
            prefetch_logits(sd - 1)
        return run

    lax.switch((off + bq - 1) // LANES, [diagonal(nc) for nc in range(1, n_sub + 1)])

    def body(it, c):
        s = sd - 1 - it
        zs = [z_scr[h] for h in range(len(heads))]
        prefetch_logits(s - 1)
        weigh(s, zs, n_sub, None)
        return c

    lax.fori_loop(0, sd, body, 0)
    o_ref[...] = acc_scr[...].astype(o_ref.dtype)


def _sb_tri():
    s_from = np.arange(LANES)[:, None]
    s_to = np.arange(LANES)[None, :]
    half = np.concatenate([(s_from > s_to), np.ones((LANES, LANES), bool)], axis=1)
    return jnp.asarray(np.concatenate([half, half], axis=0), dtype=BF16)


def _sb_attention(q, k, v, *, bq, n_sub, q_pos0):
    b_sz, tq, _ = q.shape
    tk = k.shape[1]
    width = n_sub * LANES
    assert tq % bq == 0 and tk % width == 0 and tk >= q_pos0 + tq
    assert all((q_pos0 + i * bq) % width + bq <= width for i in range(tq // bq))
    n_pair = SB_PAIRS_PER_STEP
    cols = n_pair * LANES
    kern = functools.partial(_sb_kernel, bq=bq, n_sub=n_sub, n_pair=n_pair, q_pos0=q_pos0,
                             scale=1.0 / math.sqrt(HEAD_DIM))
    return pl.pallas_call(
        kern,
        grid=(b_sz, SB_WIDTH // cols, tq // bq),
        in_specs=[
            pl.BlockSpec((None, bq, cols), lambda b, p, i: (b, i, p)),
            pl.BlockSpec((None, tk, cols), lambda b, p, i: (b, 0, p)),
            pl.BlockSpec((None, tk, cols), lambda b, p, i: (b, 0, p)),
            pl.BlockSpec((2 * LANES, 2 * LANES), lambda b, p, i: (0, 0)),
        ],
        out_specs=pl.BlockSpec((None, bq, cols), lambda b, p, i: (b, i, p)),
        out_shape=jax.ShapeDtypeStruct((b_sz, tq, SB_WIDTH), BF16),
        scratch_shapes=[pltpu.VMEM((bq, cols), F32), pltpu.VMEM((2 * n_pair, bq, LANES), F32),
                        pltpu.VMEM((2 * n_pair, bq, width), F32)],
        compiler_params=_params("parallel", "parallel", "arbitrary"),
        name="sb_attention",
    )(q, k, v, _sb_tri())


def _tri_rows():
    offs, sizes = [], []
    off = 0
    for t in range(CHUNK):
        n = SUBLANES * (t // SUBLANES + 1)
        offs.append(off)
        sizes.append(n)
        off += n
    return offs, sizes, off


_TRI_OFFS, _TRI_SIZES, _TRI_ROWS = _tri_rows()


def _mixer_consts():
    head = np.arange(HG_WIDTH) // HEAD_DIM
    same_head = (head[:, None] == head[None, :])
    sel = np.zeros((CHUNK, _TRI_ROWS), bool)
    for t in range(CHUNK):
        sel[t, _TRI_OFFS[t]:_TRI_OFFS[t] + _TRI_SIZES[t]] = True
    incl = np.arange(CHUNK)[:, None] >= np.arange(CHUNK)[None, :]
    cum = np.concatenate([incl, incl], axis=1)
    return (jnp.asarray(same_head, dtype=BF16), jnp.asarray(same_head, dtype=F32),
            jnp.asarray(sel, dtype=BF16), jnp.asarray(cum, dtype=BF16))


def _mixer_kernel(cv_ref, hg_ref, cpast_ref, cw_ref, lbl_ref, gn_ref, st0_ref,
                  bd_ref, bdf_ref, sel_ref, cum_ref,
                  conv_o_ref, hg_o_ref, cstate_ref, st_ref,
                  u_scr, b_scr, k_scr, a_scr, *, layer, tt):
    ti = pl.program_id(1)
    last = pl.num_programs(1) - 1
    pad = SUBLANES

    @pl.when(ti == 0)
    def _():
        u_scr[pad - 2:pad, :] = cpast_ref[...]
        st_ref[...] = st0_ref[...]

    cb = cv_ref[:, 0:CONV_WIDTH]
    u_scr[pad:pad + tt, :] = cv_ref[:, CONV_WIDTH:2 * CONV_WIDTH] * cv_ref[:, 2 * CONV_WIDTH:3 * CONV_WIDTH]
    y = u_scr[pad - 2:pad - 2 + tt, :] * cw_ref[0:1, :]
    y = y + u_scr[pad - 1:pad - 1 + tt, :] * cw_ref[1:2, :]
    y = y + u_scr[pad:pad + tt, :] * cw_ref[2:3, :]
    conv_o_ref[...] = (cb * y).astype(conv_o_ref.dtype)
    tail = u_scr[pad + tt - 2:pad + tt, :]
    u_scr[pad - 2:pad, :] = tail

    @pl.when(ti == last)
    def _():
        cstate_ref[...] = tail

    logits = lbl_ref[...]
    mx = jnp.max(logits, axis=0, keepdims=True)
    ex = jnp.exp(logits - mx)
    den = jnp.sum(ex, axis=0, keepdims=True)
    lb = jnp.zeros_like(den)
    for i in range(1, layer + 1):
        lb = lb + ex[i:i + 1, :] / den
    one_m_lb = 1.0 - lb

    gn = gn_ref[...]
    sub_row = lax.broadcasted_iota(jnp.int32, (SUBLANES, HG_WIDTH), 0)

    def chunk(c, carry):
        r0 = pl.multiple_of(c * CHUNK, CHUNK)
        rows = pl.ds(r0, CHUNK)
        qh = hg_ref[rows, 0:HG_WIDTH]
        a = hg_ref[rows, HG_WIDTH:2 * HG_WIDTH]
        vi = hg_ref[rows, 2 * HG_WIDTH:3 * HG_WIDTH]
        gg = hg_ref[rows, 3 * HG_WIDTH:4 * HG_WIDTH]

        e = jnp.exp(-jnp.abs(a))
        r = 1.0 / (1.0 + e)
        pos = a >= 0.0
        sig = jnp.where(pos, r, e * r)
        nsig = jnp.where(pos, e * r, r)
        f = lb + one_m_lb * sig
        g = jnp.log(jnp.maximum(f, F_FLOOR))
        kk = one_m_lb * nsig
        g_hi, g_lo = _split_bf16(g)
        b = jnp.dot(cum_ref[...], jnp.concatenate([g_hi, g_lo], axis=0),
                    preferred_element_type=F32)
        b_scr[...] = b
        k_scr[...] = kk

        for t in range(CHUNK):
            n = _TRI_SIZES[t]
            qt = hg_ref[pl.ds(r0 + t, 1), 0:HG_WIDTH]
            bt = b_scr[t:t + 1, :]
            diff = bt - b_scr[0:n, :]
            a_full = qt * jnp.exp(diff[:n - SUBLANES]) * k_scr[0:n - SUBLANES, :] if n > SUBLANES else None
            ok = sub_row <= (t % SUBLANES)
            d_last = jnp.where(ok, diff[n - SUBLANES:], 0.0)
            a_last = jnp.where(ok, qt * jnp.exp(d_last) * k_scr[n - SUBLANES:n, :], 0.0)
            if a_full is not None:
                a_scr[_TRI_OFFS[t]:_TRI_OFFS[t] + n - SUBLANES, :] = a_full
            a_scr[_TRI_OFFS[t] + n - SUBLANES:_TRI_OFFS[t] + n, :] = a_last
        scores = jnp.dot(a_scr[...].astype(BF16), bd_ref[...], preferred_element_type=F32)
        p = jnp.concatenate(
            [scores[_TRI_OFFS[t]:_TRI_OFFS[t] + _TRI_SIZES[t]] * vi[0:_TRI_SIZES[t]] for t in range(CHUNK)],
            axis=0)
        intra = jnp.dot(sel_ref[...], p.astype(BF16), preferred_element_type=F32)

        st = st_ref[...]
        inter = lax.dot_general((qh * jnp.exp(b)).astype(BF16), st.astype(BF16),
                                (((1,), (1,)), ((), ())), preferred_element_type=F32)
        b_last = b[CHUNK - 1:CHUNK, :]
        kd = kk * jnp.exp(b_last - b)
        upd = lax.dot_general(vi.astype(BF16), kd.astype(BF16), (((0,), (0,)), ((), ())),
                              preferred_element_type=F32)
        st_ref[...] = st * jnp.exp(b_last) + upd * bdf_ref[...]

        o = inter + intra
        o_hi, o_lo = _split_bf16(o * o)
        ms = (jnp.dot(o_hi, bd_ref[...], preferred_element_type=F32)
              + jnp.dot(o_lo, bd_ref[...], preferred_element_type=F32)) * (1.0 / HEAD_DIM)
        yo = o * lax.rsqrt(ms + EPS) * gn * (gg * jax.nn.sigmoid(gg))
        hg_o_ref[rows, :] = yo.astype(hg_o_ref.dtype)
        return carry

    lax.fori_loop(0, tt // CHUNK, chunk, 0)


def _mixer(cv, hg, conv_past, conv_w, lb_logits, hg_norm, st0, consts, *, layer, tt):
    b_sz, t_len, _ = cv.shape
    assert t_len % tt == 0 and tt % CHUNK == 0
    bd, bdf, sel, cum = consts
    depth = lb_logits.shape[0]
    const2 = lambda b, t: (0, 0)
    return pl.pallas_call(
        functools.partial(_mixer_kernel, layer=layer, tt=tt),
        grid=(b_sz, t_len // tt),
        in_specs=[
            pl.BlockSpec((None, tt, 3 * CONV_WIDTH), lambda b, t: (b, t, 0)),
            pl.BlockSpec((None, tt, 4 * HG_WIDTH), lambda b, t: (b, t, 0)),
            pl.BlockSpec((None, CONV_K - 1, CONV_WIDTH), lambda b, t: (b, 0, 0)),
            pl.BlockSpec((CONV_K, CONV_WIDTH), const2),
            pl.BlockSpec((depth, HG_WIDTH), const2),
            pl.BlockSpec((1, HG_WIDTH), const2),
            pl.BlockSpec((None, HG_WIDTH, HG_WIDTH), lambda b, t: (b, 0, 0)),
            pl.BlockSpec(bd.shape, const2),
            pl.BlockSpec(bdf.shape, const2),
            pl.BlockSpec(sel.shape, const2),
            pl.BlockSpec(cum.shape, const2),
        ],
        out_specs=[
            pl.BlockSpec((None, tt, CONV_WIDTH), lambda b, t: (b, t, 0)),
            pl.BlockSpec((None, tt, HG_WIDTH), lambda b, t: (b, t, 0)),
            pl.BlockSpec((None, CONV_K - 1, CONV_WIDTH), lambda b, t: (b, 0, 0)),
            pl.BlockSpec((None, HG_WIDTH, HG_WIDTH), lambda b, t: (b, 0, 0)),
        ],
        out_shape=[
            jax.ShapeDtypeStruct((b_sz, t_len, CONV_WIDTH), BF16),
            jax.ShapeDtypeStruct((b_sz, t_len, HG_WIDTH), BF16),
            jax.ShapeDtypeStruct((b_sz, CONV_K - 1, CONV_WIDTH), F32),
            jax.ShapeDtypeStruct((b_sz, HG_WIDTH, HG_WIDTH), F32),
        ],
        scratch_shapes=[
            pltpu.VMEM((tt + SUBLANES, CONV_WIDTH), F32),
            pltpu.VMEM((CHUNK, HG_WIDTH), F32),
            pltpu.VMEM((CHUNK, HG_WIDTH), F32),
            pltpu.VMEM((_TRI_ROWS, HG_WIDTH), F32),
        ],
        compiler_params=_params("parallel", "arbitrary"),
        name="mixer",
    )(cv, hg, conv_past, conv_w, lb_logits, hg_norm, st0, bd, bdf, sel, cum)


def _out_mem_kernel(x_ref, sb_ref, cv_ref, hg_ref, mk_ref, mv_ref, wo_ref, wq_ref, wmo_ref,
                    gpost1_ref, gpre2_ref, gpost2_ref, o_ref, *, scale):
    mix = jnp.dot(sb_ref[...], wo_ref[0:SB_WIDTH, :], preferred_element_type=F32)
    mix = mix + jnp.dot(cv_ref[...], wo_ref[SB_WIDTH:SB_WIDTH + CONV_WIDTH, :], preferred_element_type=F32)
    mix = mix + jnp.dot(hg_ref[...], wo_ref[SB_WIDTH + CONV_WIDTH:, :], preferred_element_type=F32)
    x = x_ref[...] + _rms(mix, gpost1_ref[...])

    h = _rms(x, gpre2_ref[...]).astype(BF16)
    q = jnp.dot(h, wq_ref[...], preferred_element_type=F32).astype(BF16)
    d = x.shape[1]
    dh = d // MEM_HEADS
    heads = []
    for hd in range(MEM_HEADS):
        cols = slice(hd * dh, (hd + 1) * dh)
        s = lax.dot_general(q[:, cols], mk_ref[:, cols], (((1,), (1,)), ((), ())),
                            preferred_element_type=F32) * scale
        s = s - jnp.max(s, axis=-1, keepdims=True)
        p = jnp.exp(s)
        p = p / jnp.sum(p, axis=-1, keepdims=True)
        heads.append(jnp.dot(p.astype(BF16), mv_ref[:, cols], preferred_element_type=F32))
    att = jnp.concatenate(heads, axis=1).astype(BF16)
    out = jnp.dot(att, wmo_ref[...], preferred_element_type=F32)
    o_ref[...] = x + _rms(out, gpost2_ref[...])


def _out_mem(x, sb, cv, hg, mem_k, mem_v, w_out, w_mq, w_mo, gpost1, gpre2, gpost2, *, tm):
    b_sz, t_len, d = x.shape
    n_mem = mem_k.shape[1]
    tok = lambda width: pl.BlockSpec((None, tm, width), lambda b, t: (b, t, 0))
    per_b = pl.BlockSpec((None, n_mem, d), lambda b, t: (b, 0, 0))
    full = lambda shape: pl.BlockSpec(shape, lambda b, t: (0, 0))
    return pl.pallas_call(
        functools.partial(_out_mem_kernel, scale=1.0 / math.sqrt(d // MEM_HEADS)),
        grid=(b_sz, t_len // tm),
        in_specs=[tok(d), tok(SB_WIDTH), tok(CONV_WIDTH), tok(HG_WIDTH), per_b, per_b,
                  full(w_out.shape), full(w_mq.shape), full(w_mo.shape),
                  full((1, d)), full((1, d)), full((1, d))],
        out_specs=tok(d),
        out_shape=jax.ShapeDtypeStruct((b_sz, t_len, d), F32),
        compiler_params=_params("parallel", "parallel"),
        name="out_mem",
    )(x, sb, cv, hg, mem_k, mem_v, w_out, w_mq, w_mo, gpost1, gpre2, gpost2)


IN_SEGMENTS = (
    (0, 0, SB_WIDTH, (BF16,)),
    (0, SB_WIDTH, SB_WIDTH, (F32, BF16)),
    (0, 2 * SB_WIDTH, SB_WIDTH, (F32, BF16)),
    (0, 3 * SB_WIDTH, 3 * CONV_WIDTH, (F32,)),
    (0, 3 * SB_WIDTH + 3 * CONV_WIDTH, 4 * HG_WIDTH, (F32,)),
)


def _sb_plan(t_len, q_pos0):
    bq = min(LANES, t_len)
    best = None
    for n_sub in (4, 3, 2, 1):
        width = n_sub * LANES
        if any((q_pos0 + i * bq) % width + bq > width for i in range(t_len // bq)):
            continue
        tk = -(-(q_pos0 + t_len) // width) * width
        if best is None or tk < best[2]:
            best = (bq, n_sub, tk)
    return best


def _state_to_kernel(s):
    b_sz = s.shape[0]
    eye = jnp.eye(HG_HEADS, dtype=s.dtype)
    return jnp.einsum('bhkv,hg->bhvgk', s, eye).reshape(b_sz, HG_WIDTH, HG_WIDTH)


def _state_from_kernel(st):
    b_sz = st.shape[0]
    st5 = st.reshape(b_sz, HG_HEADS, HEAD_DIM, HG_HEADS, HEAD_DIM)
    diag = jnp.stack([st5[:, h, :, h, :] for h in range(HG_HEADS)], axis=1)
    return jnp.swapaxes(diag, -1, -2)


def _trunk(x, q_pos0, sb_k_past, sb_v_past, conv_past, hg_past, mem_k, mem_v, w, consts):
    b_sz, t_len, d = x.shape
    depth = w["w_in"].shape[0]
    n = b_sz * t_len
    tm_ffn = _row_tile(n, 1024)
    tm_proj = _row_tile(n, 512)
    tm_tok = _row_tile(t_len, 512)
    bq, n_sub, tk = _sb_plan(t_len, q_pos0)
    row = lambda a, l, s: a[l, s][None, :]

    def keys(past, new):
        parts = [] if past is None else [past.reshape(b_sz, q_pos0, SB_WIDTH).astype(BF16)]
        parts.append(new.reshape(b_sz, t_len, SB_WIDTH))
        if tk > q_pos0 + t_len:
            parts.append(jnp.zeros((b_sz, tk - q_pos0 - t_len, SB_WIDTH), BF16))
        return parts[0] if len(parts) == 1 else jnp.concatenate(parts, axis=1)

    new_k, new_v, new_conv, new_hg = [], [], [], []
    for l in range(depth):
        x2 = _ffn(x.reshape(n, d), row(w["norm_pre"], l, 0), row(w["norm_post"], l, 0),
                  *w["ffn1"][l], tm=tm_ffn)
        sb_q, k_l, k_b, v_l, v_b, cv, hg = _norm_proj(x2, row(w["norm_pre"], l, 1), [w["w_in"][l]],
                                                      IN_SEGMENTS, tm=tm_proj)
        sb_out = _sb_attention(sb_q.reshape(b_sz, t_len, SB_WIDTH),
                               keys(None if sb_k_past is None else sb_k_past[l], k_b),
                               keys(None if sb_v_past is None else sb_v_past[l], v_b),
                               bq=bq, n_sub=n_sub, q_pos0=q_pos0)
        conv_out, hg_out, conv_state, st = _mixer(
            cv.reshape(b_sz, t_len, -1), hg.reshape(b_sz, t_len, -1), conv_past[l], w["conv_w"][l],
            w["hg_lb"], w["hg_norm"][l][None, :], _state_to_kernel(hg_past[l]), consts,
            layer=l, tt=tm_tok)
        x3 = _out_mem(x2.reshape(b_sz, t_len, d), sb_out, conv_out, hg_out, mem_k[l], mem_v[l],
                      w["w_out"][l], w["w_mq"][l], w["w_mo"][l],
                      row(w["norm_post"], l, 1), row(w["norm_pre"], l, 2), row(w["norm_post"], l, 2),
                      tm=tm_tok)
        x = _ffn(x3.reshape(n, d), row(w["norm_pre"], l, 3), row(w["norm_post"], l, 3),
                 *w["ffn2"][l], tm=tm_ffn).reshape(b_sz, t_len, d)
        new_k.append(k_l.reshape(b_sz, t_len, SB_HEADS, HEAD_DIM))
        new_v.append(v_l.reshape(b_sz, t_len, SB_HEADS, HEAD_DIM))
        new_conv.append(conv_state)
        new_hg.append(_state_from_kernel(st))
    return x, jnp.stack(new_k), jnp.stack(new_v), jnp.stack(new_conv), jnp.stack(new_hg)


def kernel(x_prompt, x_sample, mem_prompt, cache_sb_k, cache_sb_v, cache_mem_k, cache_mem_v,
           state_conv, state_hgrn, norm_pre, norm_post, ffn1_gu, ffn1_down, w_in, conv_w, hg_lb,
           hg_norm, w_out, mem_norm, w_mk, w_mv, w_mq, w_mo, ffn2_gu, ffn2_down):
    depth = w_in.shape[0]
    b_p, n_mem, d = mem_prompt.shape
    bf = lambda a: a.astype(BF16)
    d_ff = ffn1_down.shape[1]
    tf = FFN_SLAB if d_ff % FFN_SLAB == 0 else d_ff
    w = dict(norm_pre=norm_pre, norm_post=norm_post,
             ffn1=[_ffn_weights(ffn1_gu[l], ffn1_down[l], tf) for l in range(depth)],
             ffn2=[_ffn_weights(ffn2_gu[l], ffn2_down[l], tf) for l in range(depth)],
             w_in=bf(w_in), conv_w=conv_w, hg_lb=hg_lb, hg_norm=hg_norm, w_out=bf(w_out),
             w_mq=bf(w_mq), w_mo=bf(w_mo))
    consts = _mixer_consts()

    mem_flat = mem_prompt.reshape(b_p * n_mem, d)
    w_mk_b, w_mv_b = bf(w_mk), bf(w_mv)
    mem_k_l, mem_v_l, mem_k_b, mem_v_b = [], [], [], []
    for l in range(depth):
        mk, mkb, mv, mvb = _norm_proj(mem_flat, mem_norm[l][None, :], [w_mk_b[l], w_mv_b[l]],
                                      ((0, 0, d, (F32, BF16)), (1, 0, d, (F32, BF16))),
                                      tm=_row_tile(b_p * n_mem, 512))
        mem_k_l.append(mk.reshape(b_p, n_mem, MEM_HEADS, d // MEM_HEADS))
        mem_v_l.append(mv.reshape(b_p, n_mem, MEM_HEADS, d // MEM_HEADS))
        mem_k_b.append(mkb.reshape(b_p, n_mem, d))
        mem_v_b.append(mvb.reshape(b_p, n_mem, d))
    mem_k_p = jnp.stack(mem_k_l)
    mem_v_p = jnp.stack(mem_v_l)

    dt = x_prompt.dtype
    conv0 = jnp.zeros((depth, b_p, CONV_K - 1, CONV_WIDTH), dt)
    hg0 = jnp.zeros((depth, b_p, HG_HEADS, HEAD_DIM, HEAD_DIM), F32)
    y_p, sb_k_p, sb_v_p, conv_p, hg_p = _trunk(
        x_prompt, 0, None, None, conv0, hg0, mem_k_b, mem_v_b, w, consts)
    b_s = x_sample.shape[0]
    cache_k_b = [bf(cache_mem_k[l]).reshape(b_s, -1, d) for l in range(depth)]
    cache_v_b = [bf(cache_mem_v[l]).reshape(b_s, -1, d) for l in range(depth)]
    y_s, sb_k_s, sb_v_s, conv_s, hg_s = _trunk(
        x_sample, cache_sb_k.shape[2], cache_sb_k, cache_sb_v, state_conv, state_hgrn,
        cache_k_b, cache_v_b, w, consts)
    return (y_p, y_s, sb_k_p, sb_v_p, conv_p, hg_p, mem_k_p, mem_v_p, sb_k_s, sb_v_s, conv_s, hg_s)
```

```python
import functools
import math

import numpy as np
import jax
import jax.numpy as jnp
from jax import lax
from jax.experimental import pallas as pl
from jax.experimental.pallas import tpu as pltpu

F32 = jnp.float32
BF16 = jnp.bfloat16

EPS = 1e-6
LOG2_E = math.log2(math.e)
F_FLOOR = 1e-30
HEAD_DIM = 64
SB_HEADS = 8
SB_WIDTH = SB_HEADS * HEAD_DIM
CONV_WIDTH = 256
CONV_K = 3
HG_HEADS = 4
HG_WIDTH = HG_HEADS * HEAD_DIM
MEM_HEADS = 4
CHUNK = 64
SB_DEAD_LOG2 = -150.0
FFN_SLAB = 256
SB_PAIRS_PER_STEP = 2
LANES = 128
SUBLANES = 8
VMEM_LIMIT = 56 * 1024 * 1024


def _params(*sem):
    return pltpu.CompilerParams(dimension_semantics=sem, vmem_limit_bytes=VMEM_LIMIT)


def _rms(x, g):
    ms = jnp.mean(x * x, axis=-1, keepdims=True)
    return x * lax.rsqrt(ms + EPS) * g


def _split_bf16(x):
    hi = x.astype(BF16)
    lo = (x - hi.astype(F32)).astype(BF16)
    return hi, lo


def _row_tile(n, want):
    t = min(n, want)
    while n % t:
        t //= 2
    return t


def _ffn_kernel(x_ref, gpre_ref, gpost_ref, wgu_ref, wd_ref, o_ref, h_scr, acc_scr):
    nj = wd_ref.shape[0]

    def slab(h, j):
        gate = jnp.dot(h, wgu_ref[0, j], preferred_element_type=F32)
        up = jnp.dot(h, wgu_ref[1, j], preferred_element_type=F32)
        act = (gate * jax.nn.sigmoid(gate) * up).astype(BF16)
        return jnp.dot(act, wd_ref[j], preferred_element_type=F32)

    h = _rms(x_ref[...], gpre_ref[...]).astype(BF16)
    h_scr[...] = h
    acc_scr[...] = slab(h, 0)

    def middle(j, c):
        acc_scr[...] += slab(h_scr[...], j)
        return c

    lax.fori_loop(1, nj - 1, middle, 0)
    total = acc_scr[...] + slab(h_scr[...], nj - 1) if nj > 1 else acc_scr[...]
    o_ref[...] = x_ref[...] + 0.5 * _rms(total, gpost_ref[...])


def _ffn_weights(w_gu, w_down, tf):
    d, two_ff = w_gu.shape
    nj = two_ff // 2 // tf
    return (w_gu.astype(BF16).reshape(d, 2, nj, tf).transpose(1, 2, 0, 3),
            w_down.astype(BF16).reshape(nj, tf, d))


def _ffn(x, gpre, gpost, w_gu, w_down, *, tm):
    n, d = x.shape
    resident = lambda a: pl.BlockSpec(a.shape, lambda i: (0,) * a.ndim, pipeline_mode=pl.Buffered(1))
    return pl.pallas_call(
        _ffn_kernel,
        grid=(n // tm,),
        in_specs=[
            pl.BlockSpec((tm, d), lambda i: (i, 0)),
            pl.BlockSpec((1, d), lambda i: (0, 0)),
            pl.BlockSpec((1, d), lambda i: (0, 0)),
            resident(w_gu),
            resident(w_down),
        ],
        out_specs=pl.BlockSpec((tm, d), lambda i: (i, 0)),
        out_shape=jax.ShapeDtypeStruct((n, d), F32),
        scratch_shapes=[pltpu.VMEM((tm, d), BF16), pltpu.VMEM((tm, d), F32)],
        compiler_params=_params("parallel"),
        name="ffn",
    )(x, gpre, gpost, w_gu, w_down)


def _norm_proj_kernel(*refs, n_w, segments):
    x_ref, g_ref = refs[0], refs[1]
    w_refs = refs[2:2 + n_w]
    o_refs = refs[2 + n_w:]
    h = _rms(x_ref[...], g_ref[...]).astype(BF16)
    o_iter = iter(o_refs)
    for wi, off, width, dtypes in segments:
        res = jnp.dot(h, w_refs[wi][:, off:off + width], preferred_element_type=F32)
        for _ in dtypes:
            o_ref = next(o_iter)
            o_ref[...] = res.astype(o_ref.dtype)


def _norm_proj(x, g, weights, segments, *, tm):
    n, d = x.shape
    in_specs = [pl.BlockSpec((tm, d), lambda i: (i, 0)), pl.BlockSpec((1, d), lambda i: (0, 0))]
    in_specs += [pl.BlockSpec(w.shape, lambda i: (0, 0)) for w in weights]
    out_specs = [pl.BlockSpec((tm, width), lambda i: (i, 0)) for (_, _, width, dts) in segments for _ in dts]
    out_shape = [jax.ShapeDtypeStruct((n, width), dt) for (_, _, width, dts) in segments for dt in dts]
    return pl.pallas_call(
        functools.partial(_norm_proj_kernel, n_w=len(weights), segments=tuple(segments)),
        grid=(n // tm,),
        in_specs=in_specs,
        out_specs=out_specs,
        out_shape=out_shape,
        compiler_params=_params("parallel"),
        name="norm_proj",
    )(x, g, *weights)


def _sb_kernel(q_ref, k_ref, v_ref, tri_ref, o_ref, acc_scr, carry_scr, z_scr, *, bq, n_sub, n_pair, q_pos0, scale):
    width = n_sub * LANES
    pos0 = q_pos0 + pl.program_id(2) * bq
    sd = (pos0 + bq - 1) // width
    off = pos0 - sd * width
    lane = lax.broadcasted_iota(jnp.int32, (1, LANES), 1)
    even = lane < HEAD_DIM
    heads = [(pp, hh) for pp in range(n_pair) for hh in range(2)]
    pair_cols = lambda pp: slice(pp * LANES, (pp + 1) * LANES)
    log2_scale = scale * LOG2_E

    def split_pair(x):
        zero = jnp.zeros_like(x)
        return jnp.where(even, x, zero), jnp.where(even, zero, x)

    q_heads = [split_pair(q_ref[:, pair_cols(pp)]) for pp in range(n_pair)]
    acc_scr[...] = jnp.zeros_like(acc_scr)
    carry_scr[...] = jnp.zeros_like(carry_scr)

    def logits(s, nc):
        start = pl.multiple_of(s * width, width)
        out = []
        for pp, hh in heads:
            kb = k_ref[pl.ds(start, nc * LANES), pair_cols(pp)]
            out.append(lax.dot_general(q_heads[pp][hh], kb, (((1,), (1,)), ((), ())),
                                       preferred_element_type=F32) * log2_scale)
        return out

    def weigh(s, zs, nc, last_off):
        start = pl.multiple_of(s * width, width)
        if last_off is not None:
            row = lax.broadcasted_iota(jnp.int32, (bq, LANES), 0)
            col = lax.broadcasted_iota(jnp.int32, (bq, LANES), 1)
            strict = col < row + last_off
        for pp in range(n_pair):
            v_heads = split_pair(v_ref[pl.ds(start, nc * LANES), pair_cols(pp)])
            out = acc_scr[:, pair_cols(pp)]
            for hh in range(2):
                h = 2 * pp + hh
                z = zs[h]
                mn = jnp.minimum(z, 0.0)
                nz = mn - z
                sp = jnp.log2(1.0 + jnp.exp2(mn + nz))
                log_beta = mn - sp
                log_keep = nz - sp
                carry = carry_scr[h]
                ws = [None] * nc
                for c in reversed(range(nc)):
                    cols = slice(c * LANES, (c + 1) * LANES)
                    masked = last_off is not None and c == nc - 1
                    lk = jnp.where(strict, log_keep[:, cols], 0.0) if masked else log_keep[:, cols]
                    hi, lo = _split_bf16(lk)
                    sums = jnp.dot(jnp.concatenate([hi, lo], axis=1), tri_ref[...], preferred_element_type=F32)
                    arg = log_beta[:, cols] + sums[:, :LANES] + carry
                    if masked:
                        w = jnp.where(strict, jnp.exp2(jnp.where(strict, arg, 0.0)), 0.0)
                    else:
                        w = jnp.exp2(arg)
                    ws[c] = w.astype(BF16)
                    carry = carry + sums[:, LANES:]
                carry_scr[h] = carry
                w_all = ws[0] if nc == 1 else jnp.concatenate(ws, axis=1)
                out = out + jnp.dot(w_all, v_heads[hh], preferred_element_type=F32)
            acc_scr[:, pair_cols(pp)] = out

    def prefetch_logits(s):
        zs = logits(jnp.maximum(s, 0), n_sub)
        for h in range(len(heads)):
            z_scr[h] = zs[h]

    def diagonal(nc):
        def run():
            weigh(sd, logits(sd, nc), nc, off - (nc - 1) * LANES)
            prefetch_logits(sd - 1)
        return run

    lax.switch((off + bq - 1) // LANES, [diagonal(nc) for nc in range(1, n_sub + 1)])

    def more_to_do(blocks_left):
        live = jnp.max(carry_scr[...]) > SB_DEAD_LOG2
        return jnp.logical_and(blocks_left > 0, live).astype(jnp.int32)

    def body(state):
        it, _ = state
        s = sd - 1 - it
        zs = [z_scr[h] for h in range(len(heads))]
        prefetch_logits(s - 1)
        weigh(s, zs, n_sub, None)
        return it + 1, more_to_do(s)

    lax.while_loop(lambda state: state[1] != 0, body, (jnp.int32(0), more_to_do(sd)))
    o_ref[...] = acc_scr[...].astype(o_ref.dtype)


def _sb_tri():
    s_from = np.arange(LANES)[:, None]
    s_to = np.arange(LANES)[None, :]
    half = np.concatenate([(s_from > s_to), np.ones((LANES, LANES), bool)], axis=1)
    return jnp.asarray(np.concatenate([half, half], axis=0), dtype=BF16)


def _sb_attention(q, k, v, *, bq, n_sub, q_pos0):
    b_sz, tq, _ = q.shape
    tk = k.shape[1]
    width = n_sub * LANES
    assert tq % bq == 0 and tk % width == 0 and tk >= q_pos0 + tq
    assert all((q_pos0 + i * bq) % width + bq <= width for i in range(tq // bq))
    n_pair = SB_PAIRS_PER_STEP
    cols = n_pair * LANES
    kern = functools.partial(_sb_kernel, bq=bq, n_sub=n_sub, n_pair=n_pair, q_pos0=q_pos0,
                             scale=1.0 / math.sqrt(HEAD_DIM))
    return pl.pallas_call(
        kern,
        grid=(b_sz, SB_WIDTH // cols, tq // bq),
        in_specs=[
            pl.BlockSpec((None, bq, cols), lambda b, p, i: (b, i, p)),
            pl.BlockSpec((None, tk, cols), lambda b, p, i: (b, 0, p)),
            pl.BlockSpec((None, tk, cols), lambda b, p, i: (b, 0, p)),
            pl.BlockSpec((2 * LANES, 2 * LANES), lambda b, p, i: (0, 0)),
        ],
        out_specs=pl.BlockSpec((None, bq, cols), lambda b, p, i: (b, i, p)),
        out_shape=jax.ShapeDtypeStruct((b_sz, tq, SB_WIDTH), BF16),
        scratch_shapes=[pltpu.VMEM((bq, cols), F32), pltpu.VMEM((2 * n_pair, bq, LANES), F32),
                        pltpu.VMEM((2 * n_pair, bq, width), F32)],
        compiler_params=_params("parallel", "parallel", "arbitrary"),
        name="sb_attention",
    )(q, k, v, _sb_tri())


def _tri_rows():
    offs, sizes = [], []
    off = 0
    for t in range(CHUNK):
        n = SUBLANES * (t // SUBLANES + 1)
        offs.append(off)
        sizes.append(n)
        off += n
    return offs, sizes, off


_TRI_OFFS, _TRI_SIZES, _TRI_ROWS = _tri_rows()


def _mixer_consts():
    head = np.arange(HG_WIDTH) // HEAD_DIM
    same_head = (head[:, None] == head[None, :])
    sel = np.zeros((CHUNK, CHUNK * SUBLANES), bool)
    for t in range(CHUNK):
        sel[t, t * SUBLANES:(t + 1) * SUBLANES] = True
    incl = np.arange(CHUNK)[:, None] >= np.arange(CHUNK)[None, :]
    cum = np.concatenate([incl, incl], axis=1)
    return (jnp.asarray(same_head, dtype=BF16), jnp.asarray(same_head, dtype=F32),
            jnp.asarray(sel, dtype=BF16), jnp.asarray(cum, dtype=BF16))


def _mixer_kernel(cv_ref, hg_ref, cpast_ref, cw_ref, lbl_ref, gn_ref, st0_ref,
                  bd_ref, bdf_ref, sel_ref, cum_ref,
                  conv_o_ref, hg_o_ref, cstate_ref, st_ref,
                  u_scr, b_scr, k_scr, q_scr, a_scr, o_scr, *, layer, tt):
    ti = pl.program_id(1)
    last = pl.num_programs(1) - 1
    pad = SUBLANES

    @pl.when(ti == 0)
    def _():
        u_scr[pad - 2:pad, :] = cpast_ref[...]
        st_ref[...] = st0_ref[...]

    cb = cv_ref[:, 0:CONV_WIDTH]
    u_scr[pad:pad + tt, :] = cv_ref[:, CONV_WIDTH:2 * CONV_WIDTH] * cv_ref[:, 2 * CONV_WIDTH:3 * CONV_WIDTH]
    y = u_scr[pad - 2:pad - 2 + tt, :] * cw_ref[0:1, :]
    y = y + u_scr[pad - 1:pad - 1 + tt, :] * cw_ref[1:2, :]
    y = y + u_scr[pad:pad + tt, :] * cw_ref[2:3, :]
    conv_o_ref[...] = (cb * y).astype(conv_o_ref.dtype)
    tail = u_scr[pad + tt - 2:pad + tt, :]
    u_scr[pad - 2:pad, :] = tail

    @pl.when(ti == last)
    def _():
        cstate_ref[...] = tail

    logits = lbl_ref[...]
    mx = jnp.max(logits, axis=0, keepdims=True)
    ex = jnp.exp(logits - mx)
    den = jnp.sum(ex, axis=0, keepdims=True)
    lb = jnp.zeros_like(den)
    for i in range(1, layer + 1):
        lb = lb + ex[i:i + 1, :] / den
    one_m_lb = 1.0 - lb

    sub_row = lax.broadcasted_iota(jnp.int32, (SUBLANES, HG_WIDTH), 0)

    def chunk(c, carry):
        r0 = pl.multiple_of(c * CHUNK, CHUNK)
        rows = pl.ds(r0, CHUNK)
        qh = hg_ref[rows, 0:HG_WIDTH]
        a = hg_ref[rows, HG_WIDTH:2 * HG_WIDTH]
        vi = hg_ref[rows, 2 * HG_WIDTH:3 * HG_WIDTH]
        e = jnp.exp(-jnp.abs(a))
        r = 1.0 / (1.0 + e)
        pos = a >= 0.0
        sig = jnp.where(pos, r, e * r)
        nsig = jnp.where(pos, e * r, r)
        f = lb + one_m_lb * sig
        g = jnp.log(jnp.maximum(f, F_FLOOR)) * LOG2_E
        kk = one_m_lb * nsig
        g_hi, g_lo = _split_bf16(g)
        b = jnp.dot(cum_ref[...], jnp.concatenate([g_hi, g_lo], axis=0),
                    preferred_element_type=F32)
        b_scr[...] = b
        k_scr[...] = kk
        q_scr[...] = qh

        st = st_ref[...]
        inter = lax.dot_general((qh * jnp.exp2(b)).astype(BF16), st.astype(BF16),
                                (((1,), (1,)), ((), ())), preferred_element_type=F32)
        b_last = b[CHUNK - 1:CHUNK, :]
        kd = kk * jnp.exp2(b_last - b)
        upd = lax.dot_general(vi.astype(BF16), kd.astype(BF16), (((0,), (0,)), ((), ())),
                              preferred_element_type=F32)
        st_ref[...] = st * jnp.exp2(b_last) + upd * bdf_ref[...]

        for t in range(CHUNK):
            n = _TRI_SIZES[t]
            qt = q_scr[t:t + 1, :]
            bt = b_scr[t:t + 1, :]
            diff = bt - b_scr[0:n, :]
            a_full = qt * jnp.exp2(diff[:n - SUBLANES]) * k_scr[0:n - SUBLANES, :] if n > SUBLANES else None
            ok = sub_row <= (t % SUBLANES)
            d_last = jnp.where(ok, diff[n - SUBLANES:], 0.0)
            a_last = jnp.where(ok, qt * jnp.exp2(d_last) * k_scr[n - SUBLANES:n, :], 0.0)
            if a_full is not None:
                a_scr[_TRI_OFFS[t]:_TRI_OFFS[t] + n - SUBLANES, :] = a_full
            a_scr[_TRI_OFFS[t] + n - SUBLANES:_TRI_OFFS[t] + n, :] = a_last
        scores = jnp.dot(a_scr[...].astype(BF16), bd_ref[...], preferred_element_type=F32)
        partial = []
        for t in range(CHUNK):
            off = _TRI_OFFS[t]
            tile = scores[off:off + SUBLANES] * vi[0:SUBLANES]
            for s0 in range(SUBLANES, _TRI_SIZES[t], SUBLANES):
                tile = tile + scores[off + s0:off + s0 + SUBLANES] * vi[s0:s0 + SUBLANES]
            partial.append(tile)
        intra = jnp.dot(sel_ref[...], jnp.concatenate(partial, axis=0).astype(BF16),
                        preferred_element_type=F32)
        o_scr[rows, :] = inter + intra
        return carry

    lax.fori_loop(0, tt // CHUNK, chunk, 0)

    o = o_scr[...]
    gg = hg_ref[:, 3 * HG_WIDTH:4 * HG_WIDTH]
    o_hi, o_lo = _split_bf16(o * o)
    ms = (jnp.dot(o_hi, bd_ref[...], preferred_element_type=F32)
          + jnp.dot(o_lo, bd_ref[...], preferred_element_type=F32)) * (1.0 / HEAD_DIM)
    yo = o * lax.rsqrt(ms + EPS) * gn_ref[...] * (gg * jax.nn.sigmoid(gg))
    hg_o_ref[...] = yo.astype(hg_o_ref.dtype)


def _mixer(cv, hg, conv_past, conv_w, lb_logits, hg_norm, st0, consts, *, layer, tt):
    b_sz, t_len, _ = cv.shape
    assert t_len % tt == 0 and tt % CHUNK == 0
    bd, bdf, sel, cum = consts
    depth = lb_logits.shape[0]
    const2 = lambda b, t: (0, 0)
    return pl.pallas_call(
        functools.partial(_mixer_kernel, layer=layer, tt=tt),
        grid=(b_sz, t_len // tt),
        in_specs=[
            pl.BlockSpec((None, tt, 3 * CONV_WIDTH), lambda b, t: (b, t, 0)),
            pl.BlockSpec((None, tt, 4 * HG_WIDTH), lambda b, t: (b, t, 0)),
            pl.BlockSpec((None, CONV_K - 1, CONV_WIDTH), lambda b, t: (b, 0, 0)),
            pl.BlockSpec((CONV_K, CONV_WIDTH), const2),
            pl.BlockSpec((depth, HG_WIDTH), const2),
            pl.BlockSpec((1, HG_WIDTH), const2),
            pl.BlockSpec((None, HG_WIDTH, HG_WIDTH), lambda b, t: (b, 0, 0)),
            pl.BlockSpec(bd.shape, const2),
            pl.BlockSpec(bdf.shape, const2),
            pl.BlockSpec(sel.shape, const2),
            pl.BlockSpec(cum.shape, const2),
        ],
        out_specs=[
            pl.BlockSpec((None, tt, CONV_WIDTH), lambda b, t: (b, t, 0)),
            pl.BlockSpec((None, tt, HG_WIDTH), lambda b, t: (b, t, 0)),
            pl.BlockSpec((None, CONV_K - 1, CONV_WIDTH), lambda b, t: (b, 0, 0)),
            pl.BlockSpec((None, HG_WIDTH, HG_WIDTH), lambda b, t: (b, 0, 0)),
        ],
        out_shape=[
            jax.ShapeDtypeStruct((b_sz, t_len, CONV_WIDTH), BF16),
            jax.ShapeDtypeStruct((b_sz, t_len, HG_WIDTH), BF16),
            jax.ShapeDtypeStruct((b_sz, CONV_K - 1, CONV_WIDTH), F32),
            jax.ShapeDtypeStruct((b_sz, HG_WIDTH, HG_WIDTH), F32),
        ],
        scratch_shapes=[
            pltpu.VMEM((tt + SUBLANES, CONV_WIDTH), F32),
            pltpu.VMEM((CHUNK, HG_WIDTH), F32),
            pltpu.VMEM((CHUNK, HG_WIDTH), F32),
            pltpu.VMEM((CHUNK, HG_WIDTH), F32),
            pltpu.VMEM((_TRI_ROWS, HG_WIDTH), F32),
            pltpu.VMEM((tt, HG_WIDTH), F32),
        ],
        compiler_params=_params("parallel", "arbitrary"),
        name="mixer",
    )(cv, hg, conv_past, conv_w, lb_logits, hg_norm, st0, bd, bdf, sel, cum)


def _out_mem_kernel(x_ref, sb_ref, cv_ref, hg_ref, mk_ref, mv_ref, wo_ref, wq_ref, wmo_ref,
                    gpost1_ref, gpre2_ref, gpost2_ref, o_ref, *, scale):
    mix = jnp.dot(sb_ref[...], wo_ref[0:SB_WIDTH, :], preferred_element_type=F32)
    mix = mix + jnp.dot(cv_ref[...], wo_ref[SB_WIDTH:SB_WIDTH + CONV_WIDTH, :], preferred_element_type=F32)
    mix = mix + jnp.dot(hg_ref[...], wo_ref[SB_WIDTH + CONV_WIDTH:, :], preferred_element_type=F32)
    x = x_ref[...] + _rms(mix, gpost1_ref[...])

    h = _rms(x, gpre2_ref[...]).astype(BF16)
    q = jnp.dot(h, wq_ref[...], preferred_element_type=F32).astype(BF16)
    d = x.shape[1]
    dh = d // MEM_HEADS
    heads = []
    for hd in range(MEM_HEADS):
        cols = slice(hd * dh, (hd + 1) * dh)
        s = lax.dot_general(q[:, cols], mk_ref[:, cols], (((1,), (1,)), ((), ())),
                            preferred_element_type=F32) * scale
        s = s - jnp.max(s, axis=-1, keepdims=True)
        p = jnp.exp(s)
        p = p / jnp.sum(p, axis=-1, keepdims=True)
        heads.append(jnp.dot(p.astype(BF16), mv_ref[:, cols], preferred_element_type=F32))
    att = jnp.concatenate(heads, axis=1).astype(BF16)
    out = jnp.dot(att, wmo_ref[...], preferred_element_type=F32)
    o_ref[...] = x + _rms(out, gpost2_ref[...])


def _out_mem(x, sb, cv, hg, mem_k, mem_v, w_out, w_mq, w_mo, gpost1, gpre2, gpost2, *, tm):
    b_sz, t_len, d = x.shape
    n_mem = mem_k.shape[1]
    tok = lambda width: pl.BlockSpec((None, tm, width), lambda b, t: (b, t, 0))
    per_b = pl.BlockSpec((None, n_mem, d), lambda b, t: (b, 0, 0))
    full = lambda shape: pl.BlockSpec(shape, lambda b, t: (0, 0))
    return pl.pallas_call(
        functools.partial(_out_mem_kernel, scale=1.0 / math.sqrt(d // MEM_HEADS)),
        grid=(b_sz, t_len // tm),
        in_specs=[tok(d), tok(SB_WIDTH), tok(CONV_WIDTH), tok(HG_WIDTH), per_b, per_b,
                  full(w_out.shape), full(w_mq.shape), full(w_mo.shape),
                  full((1, d)), full((1, d)), full((1, d))],
        out_specs=tok(d),
        out_shape=jax.ShapeDtypeStruct((b_sz, t_len, d), F32),
        compiler_params=_params("parallel", "parallel"),
        name="out_mem",
    )(x, sb, cv, hg, mem_k, mem_v, w_out, w_mq, w_mo, gpost1, gpre2, gpost2)


IN_SEGMENTS = (
    (0, 0, SB_WIDTH, (BF16,)),
    (0, SB_WIDTH, SB_WIDTH, (F32, BF16)),
    (0, 2 * SB_WIDTH, SB_WIDTH, (F32, BF16)),
    (0, 3 * SB_WIDTH, 3 * CONV_WIDTH, (F32,)),
    (0, 3 * SB_WIDTH + 3 * CONV_WIDTH, 4 * HG_WIDTH, (F32,)),
)


def _sb_plan(t_len, q_pos0):
    bq = min(LANES, t_len)
    best = None
    for n_sub in (4, 3, 2, 1):
        width = n_sub * LANES
        if any((q_pos0 + i * bq) % width + bq > width for i in range(t_len // bq)):
            continue
        tk = -(-(q_pos0 + t_len) // width) * width
        if best is None or tk < best[2]:
            best = (bq, n_sub, tk)
    return best


def _state_to_kernel(s):
    b_sz = s.shape[0]
    eye = jnp.eye(HG_HEADS, dtype=s.dtype)
    return jnp.einsum('bhkv,hg->bhvgk', s, eye).reshape(b_sz, HG_WIDTH, HG_WIDTH)


def _state_from_kernel(st):
    b_sz = st.shape[0]
    st5 = st.reshape(b_sz, HG_HEADS, HEAD_DIM, HG_HEADS, HEAD_DIM)
    diag = jnp.stack([st5[:, h, :, h, :] for h in range(HG_HEADS)], axis=1)
    return jnp.swapaxes(diag, -1, -2)


def _trunk(x, q_pos0, sb_k_past, sb_v_past, conv_past, hg_past, mem_k, mem_v, w, consts):
    b_sz, t_len, d = x.shape
    depth = w["w_in"].shape[0]
    n = b_sz * t_len
    tm_ffn = _row_tile(n, 1024)
    tm_proj = _row_tile(n, 512)
    tm_tok = _row_tile(t_len, 512)
    bq, n_sub, tk = _sb_plan(t_len, q_pos0)
    row = lambda a, l, s: a[l, s][None, :]

    def keys(past, new):
        parts = [] if past is None else [past.reshape(b_sz, q_pos0, SB_WIDTH).astype(BF16)]
        parts.append(new.reshape(b_sz, t_len, SB_WIDTH))
        if tk > q_pos0 + t_len:
            parts.append(jnp.zeros((b_sz, tk - q_pos0 - t_len, SB_WIDTH), BF16))
        return parts[0] if len(parts) == 1 else jnp.concatenate(parts, axis=1)

    new_k, new_v, new_conv, new_hg = [], [], [], []
    for l in range(depth):
        x2 = _ffn(x.reshape(n, d), row(w["norm_pre"], l, 0), row(w["norm_post"], l, 0),
                  *w["ffn1"][l], tm=tm_ffn)
        sb_q, k_l, k_b, v_l, v_b, cv, hg = _norm_proj(x2, row(w["norm_pre"], l, 1), [w["w_in"][l]],
                                                      IN_SEGMENTS, tm=tm_proj)
        sb_out = _sb_attention(sb_q.reshape(b_sz, t_len, SB_WIDTH),
                               keys(None if sb_k_past is None else sb_k_past[l], k_b),
                               keys(None if sb_v_past is None else sb_v_past[l], v_b),
                               bq=bq, n_sub=n_sub, q_pos0=q_pos0)
        conv_out, hg_out, conv_state, st = _mixer(
            cv.reshape(b_sz, t_len, -1), hg.reshape(b_sz, t_len, -1), conv_past[l], w["conv_w"][l],
            w["hg_lb"], w["hg_norm"][l][None, :], _state_to_kernel(hg_past[l]), consts,
            layer=l, tt=tm_tok)
        x3 = _out_mem(x2.reshape(b_sz, t_len, d), sb_out, conv_out, hg_out, mem_k[l], mem_v[l],
                      w["w_out"][l], w["w_mq"][l], w["w_mo"][l],
                      row(w["norm_post"], l, 1), row(w["norm_pre"], l, 2), row(w["norm_post"], l, 2),
                      tm=tm_tok)
        x = _ffn(x3.reshape(n, d), row(w["norm_pre"], l, 3), row(w["norm_post"], l, 3),
                 *w["ffn2"][l], tm=tm_ffn).reshape(b_sz, t_len, d)
        new_k.append(k_l.reshape(b_sz, t_len, SB_HEADS, HEAD_DIM))
        new_v.append(v_l.reshape(b_sz, t_len, SB_HEADS, HEAD_DIM))
        new_conv.append(conv_state)
        new_hg.append(_state_from_kernel(st))
    return x, jnp.stack(new_k), jnp.stack(new_v), jnp.stack(new_conv), jnp.stack(new_hg)


def kernel(x_prompt, x_sample, mem_prompt, cache_sb_k, cache_sb_v, cache_mem_k, cache_mem_v,
           state_conv, state_hgrn, norm_pre, norm_post, ffn1_gu, ffn1_down, w_in, conv_w, hg_lb,
           hg_norm, w_out, mem_norm, w_mk, w_mv, w_mq, w_mo, ffn2_gu, ffn2_down):
    depth = w_in.shape[0]
    b_p, n_mem, d = mem_prompt.shape
    bf = lambda a: a.astype(BF16)
    d_ff = ffn1_down.shape[1]
    tf = FFN_SLAB if d_ff % FFN_SLAB == 0 else d_ff
    w = dict(norm_pre=norm_pre, norm_post=norm_post,
             ffn1=[_ffn_weights(ffn1_gu[l], ffn1_down[l], tf) for l in range(depth)],
             ffn2=[_ffn_weights(ffn2_gu[l], ffn2_down[l], tf) for l in range(depth)],
             w_in=bf(w_in), conv_w=conv_w, hg_lb=hg_lb, hg_norm=hg_norm, w_out=bf(w_out),
             w_mq=bf(w_mq), w_mo=bf(w_mo))
    consts = _mixer_consts()

    mem_flat = mem_prompt.reshape(b_p * n_mem, d)
    w_mk_b, w_mv_b = bf(w_mk), bf(w_mv)
    mem_k_l, mem_v_l, mem_k_b, mem_v_b = [], [], [], []
    for l in range(depth):
        mk, mkb, mv, mvb = _norm_proj(mem_flat, mem_norm[l][None, :], [w_mk_b[l], w_mv_b[l]],
                                      ((0, 0, d, (F32, BF16)), (1, 0, d, (F32, BF16))),
                                      tm=_row_tile(b_p * n_mem, 512))
        mem_k_l.append(mk.reshape(b_p, n_mem, MEM_HEADS, d // MEM_HEADS))
        mem_v_l.append(mv.reshape(b_p, n_mem, MEM_HEADS, d // MEM_HEADS))
        mem_k_b.append(mkb.reshape(b_p, n_mem, d))
        mem_v_b.append(mvb.reshape(b_p, n_mem, d))
    mem_k_p = jnp.stack(mem_k_l)
    mem_v_p = jnp.stack(mem_v_l)

    dt = x_prompt.dtype
    conv0 = jnp.zeros((depth, b_p, CONV_K - 1, CONV_WIDTH), dt)
    hg0 = jnp.zeros((depth, b_p, HG_HEADS, HEAD_DIM, HEAD_DIM), F32)
    y_p, sb_k_p, sb_v_p, conv_p, hg_p = _trunk(
        x_prompt, 0, None, None, conv0, hg0, mem_k_b, mem_v_b, w, consts)
    b_s = x_sample.shape[0]
    cache_k_b = [bf(cache_mem_k[l]).reshape(b_s, -1, d) for l in range(depth)]
    cache_v_b = [bf(cache_mem_v[l]).reshape(b_s, -1, d) for l in range(depth)]
    y_s, sb_k_s, sb_v_s, conv_s, hg_s = _trunk(
        x_sample, cache_sb_k.shape[2], cache_sb_k, cache_sb_v, state_conv, state_hgrn,
        cache_k_b, cache_v_b, w, consts)
    return (y_p, y_s, sb_k_p, sb_v_p, conv_p, hg_p, mem_k_p, mem_v_p, sb_k_s, sb_v_s, conv_s, hg_s)
```

```python
import functools
import math

import numpy as np
import jax
import jax.numpy as jnp
from jax import lax
from jax.experimental import pallas as pl
from jax.experimental.pallas import tpu as pltpu

F32 = jnp.float32
BF16 = jnp.bfloat16

EPS = 1e-6
LOG2_E = math.log2(math.e)
F_FLOOR = 1e-30
HEAD_DIM = 64
SB_HEADS = 8
SB_WIDTH = SB_HEADS * HEAD_DIM
CONV_WIDTH = 256
CONV_K = 3
HG_HEADS = 4
HG_WIDTH = HG_HEADS * HEAD_DIM
MEM_HEADS = 4
CHUNK = 64
SB_DEAD_LOG2 = -150.0
FFN_SLAB = 256
SB_PAIRS_PER_STEP = 4
LANES = 128
SUBLANES = 8
VMEM_LIMIT = 56 * 1024 * 1024


def _params(*sem):
    return pltpu.CompilerParams(dimension_semantics=sem, vmem_limit_bytes=VMEM_LIMIT)


def _rms(x, g):
    ms = jnp.mean(x * x, axis=-1, keepdims=True)
    return x * lax.rsqrt(ms + EPS) * g


def _split_bf16(x):
    hi = x.astype(BF16)
    lo = (x - hi.astype(F32)).astype(BF16)
    return hi, lo


def _row_tile(n, want):
    t = min(n, want)
    while n % t:
        t //= 2
    return t


def _ffn_kernel(x_ref, gpre_ref, gpost_ref, wgu_ref, wd_ref, o_ref, h_scr, acc_scr):
    nj = wd_ref.shape[0]

    def slab(h, j):
        gate = jnp.dot(h, wgu_ref[0, j], preferred_element_type=F32)
        up = jnp.dot(h, wgu_ref[1, j], preferred_element_type=F32)
        act = (gate * jax.nn.sigmoid(gate) * up).astype(BF16)
        return jnp.dot(act, wd_ref[j], preferred_element_type=F32)

    h = _rms(x_ref[...], gpre_ref[...]).astype(BF16)
    h_scr[...] = h
    acc_scr[...] = slab(h, 0)

    def middle(j, c):
        acc_scr[...] += slab(h_scr[...], j)
        return c

    lax.fori_loop(1, nj - 1, middle, 0)
    total = acc_scr[...] + slab(h_scr[...], nj - 1) if nj > 1 else acc_scr[...]
    o_ref[...] = x_ref[...] + 0.5 * _rms(total, gpost_ref[...])


def _ffn_weights(w_gu, w_down, tf):
    d, two_ff = w_gu.shape
    nj = two_ff // 2 // tf
    return (w_gu.astype(BF16).reshape(d, 2, nj, tf).transpose(1, 2, 0, 3),
            w_down.astype(BF16).reshape(nj, tf, d))


def _ffn(x, gpre, gpost, w_gu, w_down, *, tm):
    n, d = x.shape
    resident = lambda a: pl.BlockSpec(a.shape, lambda i: (0,) * a.ndim, pipeline_mode=pl.Buffered(1))
    return pl.pallas_call(
        _ffn_kernel,
        grid=(n // tm,),
        in_specs=[
            pl.BlockSpec((tm, d), lambda i: (i, 0)),
            pl.BlockSpec((1, d), lambda i: (0, 0)),
            pl.BlockSpec((1, d), lambda i: (0, 0)),
            resident(w_gu),
            resident(w_down),
        ],
        out_specs=pl.BlockSpec((tm, d), lambda i: (i, 0)),
        out_shape=jax.ShapeDtypeStruct((n, d), F32),
        scratch_shapes=[pltpu.VMEM((tm, d), BF16), pltpu.VMEM((tm, d), F32)],
        compiler_params=_params("parallel"),
        name="ffn",
    )(x, gpre, gpost, w_gu, w_down)


def _norm_proj_kernel(*refs, n_w, segments):
    x_ref, g_ref = refs[0], refs[1]
    w_refs = refs[2:2 + n_w]
    o_refs = refs[2 + n_w:]
    h = _rms(x_ref[...], g_ref[...]).astype(BF16)
    o_iter = iter(o_refs)
    for wi, off, width, dtypes, transposed in segments:
        res = jnp.dot(h, w_refs[wi][:, off:off + width], preferred_element_type=F32)
        if transposed:
            res = res.T
        for _ in dtypes:
            o_ref = next(o_iter)
            o_ref[...] = res.astype(o_ref.dtype)


def _norm_proj(x, g, weights, segments, *, tm, seq_len=None):
    n, d = x.shape
    in_specs = [pl.BlockSpec((tm, d), lambda i: (i, 0)), pl.BlockSpec((1, d), lambda i: (0, 0))]
    in_specs += [pl.BlockSpec(w.shape, lambda i: (0, 0)) for w in weights]
    out_specs, out_shape = [], []
    for _, _, width, dtypes, transposed in segments:
        for dt in dtypes:
            if transposed:
                tiles = seq_len // tm
                out_specs.append(pl.BlockSpec((None, width, tm), lambda i, tiles=tiles: (i // tiles, 0, i % tiles)))
                out_shape.append(jax.ShapeDtypeStruct((n // seq_len, width, seq_len), dt))
            else:
                out_specs.append(pl.BlockSpec((tm, width), lambda i: (i, 0)))
                out_shape.append(jax.ShapeDtypeStruct((n, width), dt))
    return pl.pallas_call(
        functools.partial(_norm_proj_kernel, n_w=len(weights), segments=tuple(segments)),
        grid=(n // tm,),
        in_specs=in_specs,
        out_specs=out_specs,
        out_shape=out_shape,
        compiler_params=_params("parallel"),
        name="norm_proj",
    )(x, g, *weights)


def _sb_kernel(q_ref, kt_ref, vt_ref, tri_ref, o_ref, acc_scr, carry_scr, z_scr, *, bq, n_sub, n_pair, q_pos0, scale):
    width = n_sub * LANES
    pos0 = q_pos0 + pl.program_id(2) * bq
    sd = (pos0 + bq - 1) // width
    off = pos0 - sd * width
    lane = lax.broadcasted_iota(jnp.int32, (1, LANES), 1)
    even = lane < HEAD_DIM
    heads = [(pp, hh) for pp in range(n_pair) for hh in range(2)]
    pair_cols = lambda pp: slice(pp * LANES, (pp + 1) * LANES)
    log2_scale = scale * LOG2_E

    even_rows = lax.broadcasted_iota(jnp.int32, (LANES, 1), 0) < HEAD_DIM

    def split_pair(x, is_even):
        zero = jnp.zeros_like(x)
        return jnp.where(is_even, x, zero), jnp.where(is_even, zero, x)

    q_heads = [split_pair(q_ref[:, pair_cols(pp)], even) for pp in range(n_pair)]
    acc_scr[...] = jnp.zeros_like(acc_scr)
    carry_scr[...] = jnp.zeros_like(carry_scr)

    def logits(s, nc):
        start = pl.multiple_of(s * width, width)
        out = []
        for pp, hh in heads:
            kb = kt_ref[pair_cols(pp), pl.ds(start, nc * LANES)]
            out.append(jnp.dot(q_heads[pp][hh], kb, preferred_element_type=F32) * log2_scale)
        return out

    def weigh(s, zs, nc, last_off):
        start = pl.multiple_of(s * width, width)
        if last_off is not None:
            row = lax.broadcasted_iota(jnp.int32, (bq, LANES), 0)
            col = lax.broadcasted_iota(jnp.int32, (bq, LANES), 1)
            strict = col < row + last_off
        for pp in range(n_pair):
            v_heads = split_pair(vt_ref[pair_cols(pp), pl.ds(start, nc * LANES)], even_rows)
            out = acc_scr[:, pair_cols(pp)]
            for hh in range(2):
                h = 2 * pp + hh
                z = zs[h]
                mn = jnp.minimum(z, 0.0)
                nz = mn - z
                sp = jnp.log2(1.0 + jnp.exp2(mn + nz))
                log_beta = mn - sp
                log_keep = nz - sp
                carry = carry_scr[h]
                ws = [None] * nc
                for c in reversed(range(nc)):
                    cols = slice(c * LANES, (c + 1) * LANES)
                    masked = last_off is not None and c == nc - 1
                    lk = jnp.where(strict, log_keep[:, cols], 0.0) if masked else log_keep[:, cols]
                    hi, lo = _split_bf16(lk)
                    sums = jnp.dot(jnp.concatenate([hi, lo], axis=1), tri_ref[...], preferred_element_type=F32)
                    arg = log_beta[:, cols] + sums[:, :LANES] + carry
                    if masked:
                        w = jnp.where(strict, jnp.exp2(jnp.where(strict, arg, 0.0)), 0.0)
                    else:
                        w = jnp.exp2(arg)
                    ws[c] = w.astype(BF16)
                    carry = carry + sums[:, LANES:]
                carry_scr[h] = carry
                w_all = ws[0] if nc == 1 else jnp.concatenate(ws, axis=1)
                out = out + lax.dot_general(w_all, v_heads[hh], (((1,), (1,)), ((), ())),
                                            preferred_element_type=F32)
            acc_scr[:, pair_cols(pp)] = out

    def prefetch_logits(s):
        zs = logits(jnp.maximum(s, 0), n_sub)
        for h in range(len(heads)):
            z_scr[h] = zs[h]

    def diagonal(nc):
        def run():
            weigh(sd, logits(sd, nc), nc, off - (nc - 1) * LANES)
            prefetch_logits(sd - 1)
        return run

    lax.switch((off + bq - 1) // LANES, [diagonal(nc) for nc in range(1, n_sub + 1)])

    def more_to_do(blocks_left):
        live = jnp.max(carry_scr[...]) > SB_DEAD_LOG2
        return jnp.logical_and(blocks_left > 0, live).astype(jnp.int32)

    def body(state):
        it, _ = state
        s = sd - 1 - it
        zs = [z_scr[h] for h in range(len(heads))]
        prefetch_logits(s - 1)
        weigh(s, zs, n_sub, None)
        return it + 1, more_to_do(s)

    lax.while_loop(lambda state: state[1] != 0, body, (jnp.int32(0), more_to_do(sd)))
    o_ref[...] = acc_scr[...].astype(o_ref.dtype)


def _sb_tri():
    s_from = np.arange(LANES)[:, None]
    s_to = np.arange(LANES)[None, :]
    half = np.concatenate([(s_from > s_to), np.ones((LANES, LANES), bool)], axis=1)
    return jnp.asarray(np.concatenate([half, half], axis=0), dtype=BF16)


def _sb_attention(q, kt, vt, *, bq, n_sub, q_pos0):
    b_sz, tq, _ = q.shape
    tk = kt.shape[2]
    width = n_sub * LANES
    assert tq % bq == 0 and tk % width == 0 and tk >= q_pos0 + tq
    assert all((q_pos0 + i * bq) % width + bq <= width for i in range(tq // bq))
    n_pair = SB_PAIRS_PER_STEP
    cols = n_pair * LANES
    kern = functools.partial(_sb_kernel, bq=bq, n_sub=n_sub, n_pair=n_pair, q_pos0=q_pos0,
                             scale=1.0 / math.sqrt(HEAD_DIM))
    return pl.pallas_call(
        kern,
        grid=(b_sz, SB_WIDTH // cols, tq // bq),
        in_specs=[
            pl.BlockSpec((None, bq, cols), lambda b, p, i: (b, i, p)),
            pl.BlockSpec((None, cols, tk), lambda b, p, i: (b, p, 0)),
            pl.BlockSpec((None, cols, tk), lambda b, p, i: (b, p, 0)),
            pl.BlockSpec((2 * LANES, 2 * LANES), lambda b, p, i: (0, 0)),
        ],
        out_specs=pl.BlockSpec((None, bq, cols), lambda b, p, i: (b, i, p)),
        out_shape=jax.ShapeDtypeStruct((b_sz, tq, SB_WIDTH), BF16),
        scratch_shapes=[pltpu.VMEM((bq, cols), F32), pltpu.VMEM((2 * n_pair, bq, LANES), F32),
                        pltpu.VMEM((2 * n_pair, bq, width), F32)],
        compiler_params=_params("parallel", "parallel", "arbitrary"),
        name="sb_attention",
    )(q, kt, vt, _sb_tri())


def _tri_rows():
    offs, sizes = [], []
    off = 0
    for t in range(CHUNK):
        n = SUBLANES * (t // SUBLANES + 1)
        offs.append(off)
        sizes.append(n)
        off += n
    return offs, sizes, off


_TRI_OFFS, _TRI_SIZES, _TRI_ROWS = _tri_rows()


def _mixer_consts():
    head = np.arange(HG_WIDTH) // HEAD_DIM
    same_head = (head[:, None] == head[None, :])
    sel = np.zeros((CHUNK, CHUNK * SUBLANES), bool)
    for t in range(CHUNK):
        sel[t, t * SUBLANES:(t + 1) * SUBLANES] = True
    incl = np.arange(CHUNK)[:, None] >= np.arange(CHUNK)[None, :]
    cum = np.concatenate([incl, incl], axis=1)
    return (jnp.asarray(same_head, dtype=BF16), jnp.asarray(same_head, dtype=F32),
            jnp.asarray(sel, dtype=BF16), jnp.asarray(cum, dtype=BF16))


def _mixer_kernel(cv_ref, hg_ref, cpast_ref, cw_ref, lbl_ref, gn_ref, st0_ref,
                  bd_ref, bdf_ref, sel_ref, cum_ref,
                  conv_o_ref, hg_o_ref, cstate_ref, st_ref,
                  u_scr, b_scr, k_scr, q_scr, a_scr, o_scr, *, layer, tt):
    ti = pl.program_id(1)
    last = pl.num_programs(1) - 1
    pad = SUBLANES

    @pl.when(ti == 0)
    def _():
        u_scr[pad - 2:pad, :] = cpast_ref[...]
        st_ref[...] = st0_ref[...]

    cb = cv_ref[:, 0:CONV_WIDTH]
    u_scr[pad:pad + tt, :] = cv_ref[:, CONV_WIDTH:2 * CONV_WIDTH] * cv_ref[:, 2 * CONV_WIDTH:3 * CONV_WIDTH]
    y = u_scr[pad - 2:pad - 2 + tt, :] * cw_ref[0:1, :]
    y = y + u_scr[pad - 1:pad - 1 + tt, :] * cw_ref[1:2, :]
    y = y + u_scr[pad:pad + tt, :] * cw_ref[2:3, :]
    conv_o_ref[...] = (cb * y).astype(conv_o_ref.dtype)
    tail = u_scr[pad + tt - 2:pad + tt, :]
    u_scr[pad - 2:pad, :] = tail

    @pl.when(ti == last)
    def _():
        cstate_ref[...] = tail

    logits = lbl_ref[...]
    mx = jnp.max(logits, axis=0, keepdims=True)
    ex = jnp.exp(logits - mx)
    den = jnp.sum(ex, axis=0, keepdims=True)
    lb = jnp.zeros_like(den)
    for i in range(1, layer + 1):
        lb = lb + ex[i:i + 1, :] / den
    one_m_lb = 1.0 - lb

    sub_row = lax.broadcasted_iota(jnp.int32, (SUBLANES, HG_WIDTH), 0)

    def chunk(c, carry):
        r0 = pl.multiple_of(c * CHUNK, CHUNK)
        rows = pl.ds(r0, CHUNK)
        qh = hg_ref[rows, 0:HG_WIDTH]
        a = hg_ref[rows, HG_WIDTH:2 * HG_WIDTH]
        vi = hg_ref[rows, 2 * HG_WIDTH:3 * HG_WIDTH]
        e = jnp.exp(-jnp.abs(a))
        r = 1.0 / (1.0 + e)
        pos = a >= 0.0
        sig = jnp.where(pos, r, e * r)
        nsig = jnp.where(pos, e * r, r)
        f = lb + one_m_lb * sig
        g = jnp.log(jnp.maximum(f, F_FLOOR)) * LOG2_E
        kk = one_m_lb * nsig
        g_hi, g_lo = _split_bf16(g)
        b = jnp.dot(cum_ref[...], jnp.concatenate([g_hi, g_lo], axis=0),
                    preferred_element_type=F32)
        b_scr[...] = b
        k_scr[...] = kk
        q_scr[...] = qh

        st = st_ref[...]
        inter = lax.dot_general((qh * jnp.exp2(b)).astype(BF16), st.astype(BF16),
                                (((1,), (1,)), ((), ())), preferred_element_type=F32)
        b_last = b[CHUNK - 1:CHUNK, :]
        kd = kk * jnp.exp2(b_last - b)
        upd = lax.dot_general(vi.astype(BF16), kd.astype(BF16), (((0,), (0,)), ((), ())),
                              preferred_element_type=F32)
        st_ref[...] = st * jnp.exp2(b_last) + upd * bdf_ref[...]

        for t in range(CHUNK):
            n = _TRI_SIZES[t]
            qt = q_scr[t:t + 1, :]
            bt = b_scr[t:t + 1, :]
            diff = bt - b_scr[0:n, :]
            a_full = qt * jnp.exp2(diff[:n - SUBLANES]) * k_scr[0:n - SUBLANES, :] if n > SUBLANES else None
            ok = sub_row <= (t % SUBLANES)
            d_last = jnp.where(ok, diff[n - SUBLANES:], 0.0)
            a_last = jnp.where(ok, qt * jnp.exp2(d_last) * k_scr[n - SUBLANES:n, :], 0.0)
            if a_full is not None:
                a_scr[_TRI_OFFS[t]:_TRI_OFFS[t] + n - SUBLANES, :] = a_full
            a_scr[_TRI_OFFS[t] + n - SUBLANES:_TRI_OFFS[t] + n, :] = a_last
        scores = jnp.dot(a_scr[...].astype(BF16), bd_ref[...], preferred_element_type=F32)
        partial = []
        for t in range(CHUNK):
            off = _TRI_OFFS[t]
            tile = scores[off:off + SUBLANES] * vi[0:SUBLANES]
            for s0 in range(SUBLANES, _TRI_SIZES[t], SUBLANES):
                tile = tile + scores[off + s0:off + s0 + SUBLANES] * vi[s0:s0 + SUBLANES]
            partial.append(tile)
        intra = jnp.dot(sel_ref[...], jnp.concatenate(partial, axis=0).astype(BF16),
                        preferred_element_type=F32)
        o_scr[rows, :] = inter + intra
        return carry

    lax.fori_loop(0, tt // CHUNK, chunk, 0)

    o = o_scr[...]
    gg = hg_ref[:, 3 * HG_WIDTH:4 * HG_WIDTH]
    o_hi, o_lo = _split_bf16(o * o)
    ms = (jnp.dot(o_hi, bd_ref[...], preferred_element_type=F32)
          + jnp.dot(o_lo, bd_ref[...], preferred_element_type=F32)) * (1.0 / HEAD_DIM)
    yo = o * lax.rsqrt(ms + EPS) * gn_ref[...] * (gg * jax.nn.sigmoid(gg))
    hg_o_ref[...] = yo.astype(hg_o_ref.dtype)


def _mixer(cv, hg, conv_past, conv_w, lb_logits, hg_norm, st0, consts, *, layer, tt):
    b_sz, t_len, _ = cv.shape
    assert t_len % tt == 0 and tt % CHUNK == 0
    bd, bdf, sel, cum = consts
    depth = lb_logits.shape[0]
    const2 = lambda b, t: (0, 0)
    return pl.pallas_call(
        functools.partial(_mixer_kernel, layer=layer, tt=tt),
        grid=(b_sz, t_len // tt),
        in_specs=[
            pl.BlockSpec((None, tt, 3 * CONV_WIDTH), lambda b, t: (b, t, 0)),
            pl.BlockSpec((None, tt, 4 * HG_WIDTH), lambda b, t: (b, t, 0)),
            pl.BlockSpec((None, CONV_K - 1, CONV_WIDTH), lambda b, t: (b, 0, 0)),
            pl.BlockSpec((CONV_K, CONV_WIDTH), const2),
            pl.BlockSpec((depth, HG_WIDTH), const2),
            pl.BlockSpec((1, HG_WIDTH), const2),
            pl.BlockSpec((None, HG_WIDTH, HG_WIDTH), lambda b, t: (b, 0, 0)),
            pl.BlockSpec(bd.shape, const2),
            pl.BlockSpec(bdf.shape, const2),
            pl.BlockSpec(sel.shape, const2),
            pl.BlockSpec(cum.shape, const2),
        ],
        out_specs=[
            pl.BlockSpec((None, tt, CONV_WIDTH), lambda b, t: (b, t, 0)),
            pl.BlockSpec((None, tt, HG_WIDTH), lambda b, t: (b, t, 0)),
            pl.BlockSpec((None, CONV_K - 1, CONV_WIDTH), lambda b, t: (b, 0, 0)),
            pl.BlockSpec((None, HG_WIDTH, HG_WIDTH), lambda b, t: (b, 0, 0)),
        ],
        out_shape=[
            jax.ShapeDtypeStruct((b_sz, t_len, CONV_WIDTH), BF16),
            jax.ShapeDtypeStruct((b_sz, t_len, HG_WIDTH), BF16),
            jax.ShapeDtypeStruct((b_sz, CONV_K - 1, CONV_WIDTH), F32),
            jax.ShapeDtypeStruct((b_sz, HG_WIDTH, HG_WIDTH), F32),
        ],
        scratch_shapes=[
            pltpu.VMEM((tt + SUBLANES, CONV_WIDTH), F32),
            pltpu.VMEM((CHUNK, HG_WIDTH), F32),
            pltpu.VMEM((CHUNK, HG_WIDTH), F32),
            pltpu.VMEM((CHUNK, HG_WIDTH), F32),
            pltpu.VMEM((_TRI_ROWS, HG_WIDTH), F32),
            pltpu.VMEM((tt, HG_WIDTH), F32),
        ],
        compiler_params=_params("parallel", "arbitrary"),
        name="mixer",
    )(cv, hg, conv_past, conv_w, lb_logits, hg_norm, st0, bd, bdf, sel, cum)


def _out_mem_kernel(x_ref, sb_ref, cv_ref, hg_ref, mk_ref, mv_ref, wo_ref, wq_ref, wmo_ref,
                    gpost1_ref, gpre2_ref, gpost2_ref, o_ref, *, scale):
    mix = jnp.dot(sb_ref[...], wo_ref[0:SB_WIDTH, :], preferred_element_type=F32)
    mix = mix + jnp.dot(cv_ref[...], wo_ref[SB_WIDTH:SB_WIDTH + CONV_WIDTH, :], preferred_element_type=F32)
    mix = mix + jnp.dot(hg_ref[...], wo_ref[SB_WIDTH + CONV_WIDTH:, :], preferred_element_type=F32)
    x = x_ref[...] + _rms(mix, gpost1_ref[...])

    h = _rms(x, gpre2_ref[...]).astype(BF16)
    q = jnp.dot(h, wq_ref[...], preferred_element_type=F32).astype(BF16)
    d = x.shape[1]
    dh = d // MEM_HEADS
    heads = []
    for hd in range(MEM_HEADS):
        cols = slice(hd * dh, (hd + 1) * dh)
        s = lax.dot_general(q[:, cols], mk_ref[:, cols], (((1,), (1,)), ((), ())),
                            preferred_element_type=F32) * scale
        s = s - jnp.max(s, axis=-1, keepdims=True)
        p = jnp.exp(s)
        p = p / jnp.sum(p, axis=-1, keepdims=True)
        heads.append(jnp.dot(p.astype(BF16), mv_ref[:, cols], preferred_element_type=F32))
    att = jnp.concatenate(heads, axis=1).astype(BF16)
    out = jnp.dot(att, wmo_ref[...], preferred_element_type=F32)
    o_ref[...] = x + _rms(out, gpost2_ref[...])


def _out_mem(x, sb, cv, hg, mem_k, mem_v, w_out, w_mq, w_mo, gpost1, gpre2, gpost2, *, tm):
    b_sz, t_len, d = x.shape
    n_mem = mem_k.shape[1]
    tok = lambda width: pl.BlockSpec((None, tm, width), lambda b, t: (b, t, 0))
    per_b = pl.BlockSpec((None, n_mem, d), lambda b, t: (b, 0, 0))
    full = lambda shape: pl.BlockSpec(shape, lambda b, t: (0, 0))
    return pl.pallas_call(
        functools.partial(_out_mem_kernel, scale=1.0 / math.sqrt(d // MEM_HEADS)),
        grid=(b_sz, t_len // tm),
        in_specs=[tok(d), tok(SB_WIDTH), tok(CONV_WIDTH), tok(HG_WIDTH), per_b, per_b,
                  full(w_out.shape), full(w_mq.shape), full(w_mo.shape),
                  full((1, d)), full((1, d)), full((1, d))],
        out_specs=tok(d),
        out_shape=jax.ShapeDtypeStruct((b_sz, t_len, d), F32),
        compiler_params=_params("parallel", "parallel"),
        name="out_mem",
    )(x, sb, cv, hg, mem_k, mem_v, w_out, w_mq, w_mo, gpost1, gpre2, gpost2)


def _in_segments(feature_major):
    return (
        (0, 0, SB_WIDTH, (BF16,), False),
        (0, SB_WIDTH, SB_WIDTH, (F32, BF16), feature_major),
        (0, 2 * SB_WIDTH, SB_WIDTH, (F32, BF16), feature_major),
        (0, 3 * SB_WIDTH, 3 * CONV_WIDTH, (F32,), False),
        (0, 3 * SB_WIDTH + 3 * CONV_WIDTH, 4 * HG_WIDTH, (F32,), False),
    )


def _sb_plan(t_len, q_pos0):
    bq = min(LANES, t_len)
    best = None
    for n_sub in (4, 3, 2, 1):
        width = n_sub * LANES
        if any((q_pos0 + i * bq) % width + bq > width for i in range(t_len // bq)):
            continue
        tk = -(-(q_pos0 + t_len) // width) * width
        if best is None or tk < best[2]:
            best = (bq, n_sub, tk)
    return best


def _state_to_kernel(s):
    b_sz = s.shape[0]
    eye = jnp.eye(HG_HEADS, dtype=s.dtype)
    return jnp.einsum('bhkv,hg->bhvgk', s, eye).reshape(b_sz, HG_WIDTH, HG_WIDTH)


def _state_from_kernel(st):
    b_sz = st.shape[0]
    st5 = st.reshape(b_sz, HG_HEADS, HEAD_DIM, HG_HEADS, HEAD_DIM)
    diag = jnp.stack([st5[:, h, :, h, :] for h in range(HG_HEADS)], axis=1)
    return jnp.swapaxes(diag, -1, -2)


def _trunk(x, q_pos0, sb_k_past, sb_v_past, conv_past, hg_past, mem_k, mem_v, w, consts):
    b_sz, t_len, d = x.shape
    depth = w["w_in"].shape[0]
    n = b_sz * t_len
    tm_ffn = _row_tile(n, 1024)
    tm_tok = _row_tile(t_len, 512)
    in_kernel_t = t_len % LANES == 0
    tm_proj = tm_tok if in_kernel_t else _row_tile(n, 512)
    bq, n_sub, tk = _sb_plan(t_len, q_pos0)
    row = lambda a, l, s: a[l, s][None, :]
    feature_major = lambda a: a if in_kernel_t else a.reshape(b_sz, t_len, SB_WIDTH).transpose(0, 2, 1)

    def keys(past, new):
        parts = []
        if past is not None:
            parts.append(past.transpose(0, 2, 3, 1).reshape(b_sz, SB_WIDTH, q_pos0).astype(BF16))
        parts.append(new)
        if tk > q_pos0 + t_len:
            parts.append(jnp.zeros((b_sz, SB_WIDTH, tk - q_pos0 - t_len), BF16))
        return parts[0] if len(parts) == 1 else jnp.concatenate(parts, axis=2)

    new_k, new_v, new_conv, new_hg = [], [], [], []
    for l in range(depth):
        x2 = _ffn(x.reshape(n, d), row(w["norm_pre"], l, 0), row(w["norm_post"], l, 0),
                  *w["ffn1"][l], tm=tm_ffn)
        sb_q, k_l, k_b, v_l, v_b, cv, hg = _norm_proj(x2, row(w["norm_pre"], l, 1), [w["w_in"][l]],
                                                      _in_segments(in_kernel_t), tm=tm_proj, seq_len=t_len)
        k_l, k_b, v_l, v_b = (feature_major(a) for a in (k_l, k_b, v_l, v_b))
        sb_out = _sb_attention(sb_q.reshape(b_sz, t_len, SB_WIDTH),
                               keys(None if sb_k_past is None else sb_k_past[l], k_b),
                               keys(None if sb_v_past is None else sb_v_past[l], v_b),
                               bq=bq, n_sub=n_sub, q_pos0=q_pos0)
        conv_out, hg_out, conv_state, st = _mixer(
            cv.reshape(b_sz, t_len, -1), hg.reshape(b_sz, t_len, -1), conv_past[l], w["conv_w"][l],
            w["hg_lb"], w["hg_norm"][l][None, :], _state_to_kernel(hg_past[l]), consts,
            layer=l, tt=tm_tok)
        x3 = _out_mem(x2.reshape(b_sz, t_len, d), sb_out, conv_out, hg_out, mem_k[l], mem_v[l],
                      w["w_out"][l], w["w_mq"][l], w["w_mo"][l],
                      row(w["norm_post"], l, 1), row(w["norm_pre"], l, 2), row(w["norm_post"], l, 2),
                      tm=tm_tok)
        x = _ffn(x3.reshape(n, d), row(w["norm_pre"], l, 3), row(w["norm_post"], l, 3),
                 *w["ffn2"][l], tm=tm_ffn).reshape(b_sz, t_len, d)
        new_k.append(k_l)
        new_v.append(v_l)
        new_conv.append(conv_state)
        new_hg.append(_state_from_kernel(st))

    def positions_major(layers):
        a = jnp.stack(layers).reshape(depth, b_sz, SB_HEADS, HEAD_DIM, t_len)
        return a.transpose(0, 1, 4, 2, 3)

    return x, positions_major(new_k), positions_major(new_v), jnp.stack(new_conv), jnp.stack(new_hg)


def kernel(x_prompt, x_sample, mem_prompt, cache_sb_k, cache_sb_v, cache_mem_k, cache_mem_v,
           state_conv, state_hgrn, norm_pre, norm_post, ffn1_gu, ffn1_down, w_in, conv_w, hg_lb,
           hg_norm, w_out, mem_norm, w_mk, w_mv, w_mq, w_mo, ffn2_gu, ffn2_down):
    depth = w_in.shape[0]
    b_p, n_mem, d = mem_prompt.shape
    bf = lambda a: a.astype(BF16)
    d_ff = ffn1_down.shape[1]
    tf = FFN_SLAB if d_ff % FFN_SLAB == 0 else d_ff
    w = dict(norm_pre=norm_pre, norm_post=norm_post,
             ffn1=[_ffn_weights(ffn1_gu[l], ffn1_down[l], tf) for l in range(depth)],
             ffn2=[_ffn_weights(ffn2_gu[l], ffn2_down[l], tf) for l in range(depth)],
             w_in=bf(w_in), conv_w=conv_w, hg_lb=hg_lb, hg_norm=hg_norm, w_out=bf(w_out),
             w_mq=bf(w_mq), w_mo=bf(w_mo))
    consts = _mixer_consts()

    mem_flat = mem_prompt.reshape(b_p * n_mem, d)
    w_mk_b, w_mv_b = bf(w_mk), bf(w_mv)
    mem_k_l, mem_v_l, mem_k_b, mem_v_b = [], [], [], []
    for l in range(depth):
        mk, mkb, mv, mvb = _norm_proj(mem_flat, mem_norm[l][None, :], [w_mk_b[l], w_mv_b[l]],
                                      ((0, 0, d, (F32, BF16), False), (1, 0, d, (F32, BF16), False)),
                                      tm=_row_tile(b_p * n_mem, 512))
        mem_k_l.append(mk.reshape(b_p, n_mem, MEM_HEADS, d // MEM_HEADS))
        mem_v_l.append(mv.reshape(b_p, n_mem, MEM_HEADS, d // MEM_HEADS))
        mem_k_b.append(mkb.reshape(b_p, n_mem, d))
        mem_v_b.append(mvb.reshape(b_p, n_mem, d))
    mem_k_p = jnp.stack(mem_k_l)
    mem_v_p = jnp.stack(mem_v_l)

    dt = x_prompt.dtype
    conv0 = jnp.zeros((depth, b_p, CONV_K - 1, CONV_WIDTH), dt)
    hg0 = jnp.zeros((depth, b_p, HG_HEADS, HEAD_DIM, HEAD_DIM), F32)
    y_p, sb_k_p, sb_v_p, conv_p, hg_p = _trunk(
        x_prompt, 0, None, None, conv0, hg0, mem_k_b, mem_v_b, w, consts)
    b_s = x_sample.shape[0]
    cache_k_b = [bf(cache_mem_k[l]).reshape(b_s, -1, d) for l in range(depth)]
    cache_v_b = [bf(cache_mem_v[l]).reshape(b_s, -1, d) for l in range(depth)]
    y_s, sb_k_s, sb_v_s, conv_s, hg_s = _trunk(
        x_sample, cache_sb_k.shape[2], cache_sb_k, cache_sb_v, state_conv, state_hgrn,
        cache_k_b, cache_v_b, w, consts)
    return (y_p, y_s, sb_k_p, sb_v_p, conv_p, hg_p, mem_k_p, mem_v_p, sb_k_s, sb_v_s, conv_s, hg_s)
```

```python
import functools
import math

import numpy as np
import jax
import jax.numpy as jnp
from jax import lax
from jax.experimental import pallas as pl
from jax.experimental.pallas import tpu as pltpu

F32 = jnp.float32
BF16 = jnp.bfloat16

EPS = 1e-6
LOG2_E = math.log2(math.e)
F_FLOOR = 1e-30
HEAD_DIM = 64
SB_HEADS = 8
SB_WIDTH = SB_HEADS * HEAD_DIM
CONV_WIDTH = 256
CONV_K = 3
HG_HEADS = 4
HG_WIDTH = HG_HEADS * HEAD_DIM
MEM_HEADS = 4
CHUNK = 64
SB_DEAD_LOG2 = -150.0
MIXER_CHUNKS_PER_TRIP = 8
FFN_SLABS_PER_TRIP = 3
FFN_SLAB = 256
SB_PAIRS_PER_STEP = 4
LANES = 128
SUBLANES = 8
VMEM_LIMIT = 56 * 1024 * 1024


def _params(*sem):
    return pltpu.CompilerParams(dimension_semantics=sem, vmem_limit_bytes=VMEM_LIMIT)


def _rms(x, g):
    ms = jnp.mean(x * x, axis=-1, keepdims=True)
    return x * lax.rsqrt(ms + EPS) * g


def _split_bf16(x):
    hi = x.astype(BF16)
    lo = (x - hi.astype(F32)).astype(BF16)
    return hi, lo


def _row_tile(n, want):
    t = min(n, want)
    while n % t:
        t //= 2
    return t


def _ffn_kernel(x_ref, gpre_ref, gpost_ref, wgu_ref, wd_ref, o_ref, h_scr, acc_scr):
    nj = wd_ref.shape[0]

    def slab(h, j):
        gate = jnp.dot(h, wgu_ref[0, j], preferred_element_type=F32)
        up = jnp.dot(h, wgu_ref[1, j], preferred_element_type=F32)
        act = (gate * jax.nn.sigmoid(gate) * up).astype(BF16)
        return jnp.dot(act, wd_ref[j], preferred_element_type=F32)

    h = _rms(x_ref[...], gpre_ref[...]).astype(BF16)
    h_scr[...] = h
    acc_scr[...] = slab(h, 0)

    n_mid = max(nj - 2, 0)
    per_trip = math.gcd(n_mid, FFN_SLABS_PER_TRIP) if n_mid else 1

    def middle(i, c):
        part = slab(h_scr[...], 1 + i * per_trip)
        for u in range(1, per_trip):
            part = part + slab(h_scr[...], 1 + i * per_trip + u)
        acc_scr[...] += part
        return c

    lax.fori_loop(0, n_mid // per_trip, middle, 0)
    total = acc_scr[...] + slab(h_scr[...], nj - 1) if nj > 1 else acc_scr[...]
    o_ref[...] = x_ref[...] + 0.5 * _rms(total, gpost_ref[...])


def _ffn_weights(w_gu, w_down, tf):
    d, two_ff = w_gu.shape
    nj = two_ff // 2 // tf
    return (w_gu.astype(BF16).reshape(d, 2, nj, tf).transpose(1, 2, 0, 3),
            w_down.astype(BF16).reshape(nj, tf, d))


def _ffn(x, gpre, gpost, w_gu, w_down, *, tm):
    n, d = x.shape
    resident = lambda a: pl.BlockSpec(a.shape, lambda i: (0,) * a.ndim, pipeline_mode=pl.Buffered(1))
    return pl.pallas_call(
        _ffn_kernel,
        grid=(n // tm,),
        in_specs=[
            pl.BlockSpec((tm, d), lambda i: (i, 0)),
            pl.BlockSpec((1, d), lambda i: (0, 0)),
            pl.BlockSpec((1, d), lambda i: (0, 0)),
            resident(w_gu),
            resident(w_down),
        ],
        out_specs=pl.BlockSpec((tm, d), lambda i: (i, 0)),
        out_shape=jax.ShapeDtypeStruct((n, d), F32),
        scratch_shapes=[pltpu.VMEM((tm, d), BF16), pltpu.VMEM((tm, d), F32)],
        compiler_params=_params("parallel"),
        name="ffn",
    )(x, gpre, gpost, w_gu, w_down)


def _norm_proj_kernel(*refs, n_w, segments):
    x_ref, g_ref = refs[0], refs[1]
    w_refs = refs[2:2 + n_w]
    o_refs = refs[2 + n_w:]
    h = _rms(x_ref[...], g_ref[...]).astype(BF16)
    o_iter = iter(o_refs)
    for wi, off, width, dtypes, transposed in segments:
        res = jnp.dot(h, w_refs[wi][:, off:off + width], preferred_element_type=F32)
        if transposed:
            res = res.T
        for _ in dtypes:
            o_ref = next(o_iter)
            o_ref[...] = res.astype(o_ref.dtype)


def _norm_proj(x, g, weights, segments, *, tm, seq_len=None):
    n, d = x.shape
    in_specs = [pl.BlockSpec((tm, d), lambda i: (i, 0)), pl.BlockSpec((1, d), lambda i: (0, 0))]
    in_specs += [pl.BlockSpec(w.shape, lambda i: (0, 0)) for w in weights]
    out_specs, out_shape = [], []
    for _, _, width, dtypes, transposed in segments:
        for dt in dtypes:
            if transposed:
                tiles = seq_len // tm
                out_specs.append(pl.BlockSpec((None, width, tm), lambda i, tiles=tiles: (i // tiles, 0, i % tiles)))
                out_shape.append(jax.ShapeDtypeStruct((n // seq_len, width, seq_len), dt))
            else:
                out_specs.append(pl.BlockSpec((tm, width), lambda i: (i, 0)))
                out_shape.append(jax.ShapeDtypeStruct((n, width), dt))
    return pl.pallas_call(
        functools.partial(_norm_proj_kernel, n_w=len(weights), segments=tuple(segments)),
        grid=(n // tm,),
        in_specs=in_specs,
        out_specs=out_specs,
        out_shape=out_shape,
        compiler_params=_params("parallel"),
        name="norm_proj",
    )(x, g, *weights)


def _sb_kernel(q_ref, kt_ref, vt_ref, tri_ref, o_ref, acc_scr, carry_scr, z_scr, *, bq, n_sub, n_pair, q_pos0, scale):
    width = n_sub * LANES
    pos0 = q_pos0 + pl.program_id(2) * bq
    sd = (pos0 + bq - 1) // width
    off = pos0 - sd * width
    lane = lax.broadcasted_iota(jnp.int32, (1, LANES), 1)
    even = lane < HEAD_DIM
    heads = [(pp, hh) for pp in range(n_pair) for hh in range(2)]
    pair_cols = lambda pp: slice(pp * LANES, (pp + 1) * LANES)
    log2_scale = scale * LOG2_E

    even_rows = lax.broadcasted_iota(jnp.int32, (LANES, 1), 0) < HEAD_DIM

    def split_pair(x, is_even):
        zero = jnp.zeros_like(x)
        return jnp.where(is_even, x, zero), jnp.where(is_even, zero, x)

    q_heads = [split_pair(q_ref[:, pair_cols(pp)], even) for pp in range(n_pair)]
    acc_scr[...] = jnp.zeros_like(acc_scr)
    carry_scr[...] = jnp.zeros_like(carry_scr)

    def logits(s, nc):
        start = pl.multiple_of(s * width, width)
        out = []
        for pp, hh in heads:
            kb = kt_ref[pair_cols(pp), pl.ds(start, nc * LANES)]
            out.append(jnp.dot(q_heads[pp][hh], kb, preferred_element_type=F32) * log2_scale)
        return out

    def weigh(s, zs, nc, last_off):
        start = pl.multiple_of(s * width, width)
        if last_off is not None:
            row = lax.broadcasted_iota(jnp.int32, (bq, LANES), 0)
            col = lax.broadcasted_iota(jnp.int32, (bq, LANES), 1)
            strict = col < row + last_off
        for pp in range(n_pair):
            v_heads = split_pair(vt_ref[pair_cols(pp), pl.ds(start, nc * LANES)], even_rows)
            out = acc_scr[:, pair_cols(pp)]
            for hh in range(2):
                h = 2 * pp + hh
                z = zs[h]
                mn = jnp.minimum(z, 0.0)
                nz = mn - z
                sp = jnp.log2(1.0 + jnp.exp2(mn + nz))
                log_beta = mn - sp
                log_keep = nz - sp
                carry = carry_scr[h]
                ws = [None] * nc
                for c in reversed(range(nc)):
                    cols = slice(c * LANES, (c + 1) * LANES)
                    masked = last_off is not None and c == nc - 1
                    lk = jnp.where(strict, log_keep[:, cols], 0.0) if masked else log_keep[:, cols]
                    hi, lo = _split_bf16(lk)
                    sums = jnp.dot(jnp.concatenate([hi, lo], axis=1), tri_ref[...], preferred_element_type=F32)
                    arg = log_beta[:, cols] + sums[:, :LANES] + carry
                    if masked:
                        w = jnp.where(strict, jnp.exp2(jnp.where(strict, arg, 0.0)), 0.0)
                    else:
                        w = jnp.exp2(arg)
                    ws[c] = w.astype(BF16)
                    carry = carry + sums[:, LANES:]
                carry_scr[h] = carry
                w_all = ws[0] if nc == 1 else jnp.concatenate(ws, axis=1)
                out = out + lax.dot_general(w_all, v_heads[hh], (((1,), (1,)), ((), ())),
                                            preferred_element_type=F32)
            acc_scr[:, pair_cols(pp)] = out

    def prefetch_logits(s):
        zs = logits(jnp.maximum(s, 0), n_sub)
        for h in range(len(heads)):
            z_scr[h] = zs[h]

    def diagonal(nc):
        def run():
            weigh(sd, logits(sd, nc), nc, off - (nc - 1) * LANES)
            prefetch_logits(sd - 1)
        return run

    lax.switch((off + bq - 1) // LANES, [diagonal(nc) for nc in range(1, n_sub + 1)])

    def more_to_do(blocks_left):
        live = jnp.max(carry_scr[...]) > SB_DEAD_LOG2
        return jnp.logical_and(blocks_left > 0, live).astype(jnp.int32)

    def body(state):
        it, _ = state
        s = sd - 1 - it
        zs = [z_scr[h] for h in range(len(heads))]
        prefetch_logits(s - 1)
        weigh(s, zs, n_sub, None)
        return it + 1, more_to_do(s)

    lax.while_loop(lambda state: state[1] != 0, body, (jnp.int32(0), more_to_do(sd)))
    o_ref[...] = acc_scr[...].astype(o_ref.dtype)


def _sb_tri():
    s_from = np.arange(LANES)[:, None]
    s_to = np.arange(LANES)[None, :]
    half = np.concatenate([(s_from > s_to), np.ones((LANES, LANES), bool)], axis=1)
    return jnp.asarray(np.concatenate([half, half], axis=0), dtype=BF16)


def _sb_attention(q, kt, vt, *, bq, n_sub, q_pos0):
    b_sz, tq, _ = q.shape
    tk = kt.shape[2]
    width = n_sub * LANES
    assert tq % bq == 0 and tk % width == 0 and tk >= q_pos0 + tq
    assert all((q_pos0 + i * bq) % width + bq <= width for i in range(tq // bq))
    n_pair = SB_PAIRS_PER_STEP
    cols = n_pair * LANES
    kern = functools.partial(_sb_kernel, bq=bq, n_sub=n_sub, n_pair=n_pair, q_pos0=q_pos0,
                             scale=1.0 / math.sqrt(HEAD_DIM))
    return pl.pallas_call(
        kern,
        grid=(b_sz, SB_WIDTH // cols, tq // bq),
        in_specs=[
            pl.BlockSpec((None, bq, cols), lambda b, p, i: (b, i, p)),
            pl.BlockSpec((None, cols, tk), lambda b, p, i: (b, p, 0)),
            pl.BlockSpec((None, cols, tk), lambda b, p, i: (b, p, 0)),
            pl.BlockSpec((2 * LANES, 2 * LANES), lambda b, p, i: (0, 0)),
        ],
        out_specs=pl.BlockSpec((None, bq, cols), lambda b, p, i: (b, i, p)),
        out_shape=jax.ShapeDtypeStruct((b_sz, tq, SB_WIDTH), BF16),
        scratch_shapes=[pltpu.VMEM((bq, cols), F32), pltpu.VMEM((2 * n_pair, bq, LANES), F32),
                        pltpu.VMEM((2 * n_pair, bq, width), F32)],
        compiler_params=_params("parallel", "parallel", "arbitrary"),
        name="sb_attention",
    )(q, kt, vt, _sb_tri())


def _tri_rows():
    offs, sizes = [], []
    off = 0
    for t in range(CHUNK):
        n = SUBLANES * (t // SUBLANES + 1)
        offs.append(off)
        sizes.append(n)
        off += n
    return offs, sizes, off


_TRI_OFFS, _TRI_SIZES, _TRI_ROWS = _tri_rows()


def _mixer_consts():
    head = np.arange(HG_WIDTH) // HEAD_DIM
    same_head = (head[:, None] == head[None, :])
    sel = np.zeros((CHUNK, CHUNK * SUBLANES), bool)
    for t in range(CHUNK):
        sel[t, t * SUBLANES:(t + 1) * SUBLANES] = True
    incl = np.arange(CHUNK)[:, None] >= np.arange(CHUNK)[None, :]
    cum = np.concatenate([incl, incl], axis=1)
    return (jnp.asarray(same_head, dtype=BF16), jnp.asarray(same_head, dtype=F32),
            jnp.asarray(sel, dtype=BF16), jnp.asarray(cum, dtype=BF16))


def _mixer_kernel(cv_ref, hg_ref, cpast_ref, cw_ref, lbl_ref, gn_ref, st0_ref,
                  bd_ref, bdf_ref, sel_ref, cum_ref,
                  conv_o_ref, hg_o_ref, cstate_ref, st_ref,
                  u_scr, b_scr, k_scr, q_scr, a_scr, o_scr, *, layer, tt):
    ti = pl.program_id(1)
    last = pl.num_programs(1) - 1
    pad = SUBLANES

    @pl.when(ti == 0)
    def _():
        u_scr[pad - 2:pad, :] = cpast_ref[...]
        st_ref[...] = st0_ref[...]

    cb = cv_ref[:, 0:CONV_WIDTH]
    u_scr[pad:pad + tt, :] = cv_ref[:, CONV_WIDTH:2 * CONV_WIDTH] * cv_ref[:, 2 * CONV_WIDTH:3 * CONV_WIDTH]
    y = u_scr[pad - 2:pad - 2 + tt, :] * cw_ref[0:1, :]
    y = y + u_scr[pad - 1:pad - 1 + tt, :] * cw_ref[1:2, :]
    y = y + u_scr[pad:pad + tt, :] * cw_ref[2:3, :]
    conv_o_ref[...] = (cb * y).astype(conv_o_ref.dtype)
    tail = u_scr[pad + tt - 2:pad + tt, :]
    u_scr[pad - 2:pad, :] = tail

    @pl.when(ti == last)
    def _():
        cstate_ref[...] = tail

    logits = lbl_ref[...]
    mx = jnp.max(logits, axis=0, keepdims=True)
    ex = jnp.exp(logits - mx)
    den = jnp.sum(ex, axis=0, keepdims=True)
    lb = jnp.zeros_like(den)
    for i in range(1, layer + 1):
        lb = lb + ex[i:i + 1, :] / den
    one_m_lb = 1.0 - lb

    sub_row = lax.broadcasted_iota(jnp.int32, (SUBLANES, HG_WIDTH), 0)

    def chunk_rows(c):
        return pl.ds(pl.multiple_of(c * CHUNK, CHUNK), CHUNK)

    def gates(c, par):
        rows = chunk_rows(c)
        qh = hg_ref[rows, 0:HG_WIDTH]
        a = hg_ref[rows, HG_WIDTH:2 * HG_WIDTH]
        vi = hg_ref[rows, 2 * HG_WIDTH:3 * HG_WIDTH]
        e = jnp.exp(-jnp.abs(a))
        r = 1.0 / (1.0 + e)
        pos = a >= 0.0
        sig = jnp.where(pos, r, e * r)
        nsig = jnp.where(pos, e * r, r)
        f = lb + one_m_lb * sig
        g = jnp.log(jnp.maximum(f, F_FLOOR)) * LOG2_E
        kk = one_m_lb * nsig
        g_hi, g_lo = _split_bf16(g)
        b = jnp.dot(cum_ref[...], jnp.concatenate([g_hi, g_lo], axis=0),
                    preferred_element_type=F32)
        b_scr[par] = b
        k_scr[par] = kk
        q_scr[par] = qh
        st = st_ref[...]
        inter = lax.dot_general((qh * jnp.exp2(b)).astype(BF16), st.astype(BF16),
                                (((1,), (1,)), ((), ())), preferred_element_type=F32)
        b_last = b[CHUNK - 1:CHUNK, :]
        kd = kk * jnp.exp2(b_last - b)
        upd = lax.dot_general(vi.astype(BF16), kd.astype(BF16), (((0,), (0,)), ((), ())),
                              preferred_element_type=F32)
        st_ref[...] = st * jnp.exp2(b_last) + upd * bdf_ref[...]
        o_scr[rows, :] = inter

    def intra(c, par):
        b_buf, k_buf, q_buf, a_buf = b_scr.at[par], k_scr.at[par], q_scr.at[par], a_scr.at[par]
        rows = chunk_rows(c)
        vi = hg_ref[rows, 2 * HG_WIDTH:3 * HG_WIDTH]
        for t in range(CHUNK):
            n = _TRI_SIZES[t]
            qt = q_buf[t:t + 1, :]
            bt = b_buf[t:t + 1, :]
            diff = bt - b_buf[0:n, :]
            a_full = qt * jnp.exp2(diff[:n - SUBLANES]) * k_buf[0:n - SUBLANES, :] if n > SUBLANES else None
            ok = sub_row <= (t % SUBLANES)
            d_last = jnp.where(ok, diff[n - SUBLANES:], 0.0)
            a_last = jnp.where(ok, qt * jnp.exp2(d_last) * k_buf[n - SUBLANES:n, :], 0.0)
            if a_full is not None:
                a_buf[_TRI_OFFS[t]:_TRI_OFFS[t] + n - SUBLANES, :] = a_full
            a_buf[_TRI_OFFS[t] + n - SUBLANES:_TRI_OFFS[t] + n, :] = a_last
        scores = jnp.dot(a_buf[...].astype(BF16), bd_ref[...], preferred_element_type=F32)
        partial = []
        for t in range(CHUNK):
            off = _TRI_OFFS[t]
            tile = scores[off:off + SUBLANES] * vi[0:SUBLANES]
            for s0 in range(SUBLANES, _TRI_SIZES[t], SUBLANES):
                tile = tile + scores[off + s0:off + s0 + SUBLANES] * vi[s0:s0 + SUBLANES]
            partial.append(tile)
        o_scr[rows, :] += jnp.dot(sel_ref[...], jnp.concatenate(partial, axis=0).astype(BF16),
                                  preferred_element_type=F32)

    n_chunks = tt // CHUNK
    per_trip = math.gcd(n_chunks, MIXER_CHUNKS_PER_TRIP)

    def trip(i, carry):
        first = i * per_trip
        gates(first, 0)
        for u in range(per_trip):
            if u + 1 < per_trip:
                gates(first + u + 1, (u + 1) % 2)
            intra(first + u, u % 2)
        return carry

    lax.fori_loop(0, n_chunks // per_trip, trip, 0)

    o = o_scr[...]
    gg = hg_ref[:, 3 * HG_WIDTH:4 * HG_WIDTH]
    o_hi, o_lo = _split_bf16(o * o)
    ms = (jnp.dot(o_hi, bd_ref[...], preferred_element_type=F32)
          + jnp.dot(o_lo, bd_ref[...], preferred_element_type=F32)) * (1.0 / HEAD_DIM)
    yo = o * lax.rsqrt(ms + EPS) * gn_ref[...] * (gg * jax.nn.sigmoid(gg))
    hg_o_ref[...] = yo.astype(hg_o_ref.dtype)


def _mixer(cv, hg, conv_past, conv_w, lb_logits, hg_norm, st0, consts, *, layer, tt):
    b_sz, t_len, _ = cv.shape
    assert t_len % tt == 0 and tt % CHUNK == 0
    bd, bdf, sel, cum = consts
    depth = lb_logits.shape[0]
    const2 = lambda b, t: (0, 0)
    return pl.pallas_call(
        functools.partial(_mixer_kernel, layer=layer, tt=tt),
        grid=(b_sz, t_len // tt),
        in_specs=[
            pl.BlockSpec((None, tt, 3 * CONV_WIDTH), lambda b, t: (b, t, 0)),
            pl.BlockSpec((None, tt, 4 * HG_WIDTH), lambda b, t: (b, t, 0)),
            pl.BlockSpec((None, CONV_K - 1, CONV_WIDTH), lambda b, t: (b, 0, 0)),
            pl.BlockSpec((CONV_K, CONV_WIDTH), const2),
            pl.BlockSpec((depth, HG_WIDTH), const2),
            pl.BlockSpec((1, HG_WIDTH), const2),
            pl.BlockSpec((None, HG_WIDTH, HG_WIDTH), lambda b, t: (b, 0, 0)),
            pl.BlockSpec(bd.shape, const2),
            pl.BlockSpec(bdf.shape, const2),
            pl.BlockSpec(sel.shape, const2),
            pl.BlockSpec(cum.shape, const2),
        ],
        out_specs=[
            pl.BlockSpec((None, tt, CONV_WIDTH), lambda b, t: (b, t, 0)),
            pl.BlockSpec((None, tt, HG_WIDTH), lambda b, t: (b, t, 0)),
            pl.BlockSpec((None, CONV_K - 1, CONV_WIDTH), lambda b, t: (b, 0, 0)),
            pl.BlockSpec((None, HG_WIDTH, HG_WIDTH), lambda b, t: (b, 0, 0)),
        ],
        out_shape=[
            jax.ShapeDtypeStruct((b_sz, t_len, CONV_WIDTH), BF16),
            jax.ShapeDtypeStruct((b_sz, t_len, HG_WIDTH), BF16),
            jax.ShapeDtypeStruct((b_sz, CONV_K - 1, CONV_WIDTH), F32),
            jax.ShapeDtypeStruct((b_sz, HG_WIDTH, HG_WIDTH), F32),
        ],
        scratch_shapes=[
            pltpu.VMEM((tt + SUBLANES, CONV_WIDTH), F32),
            pltpu.VMEM((2, CHUNK, HG_WIDTH), F32),
            pltpu.VMEM((2, CHUNK, HG_WIDTH), F32),
            pltpu.VMEM((2, CHUNK, HG_WIDTH), F32),
            pltpu.VMEM((2, _TRI_ROWS, HG_WIDTH), F32),
            pltpu.VMEM((tt, HG_WIDTH), F32),
        ],
        compiler_params=_params("parallel", "arbitrary"),
        name="mixer",
    )(cv, hg, conv_past, conv_w, lb_logits, hg_norm, st0, bd, bdf, sel, cum)


def _out_mem_kernel(x_ref, sb_ref, cv_ref, hg_ref, mk_ref, mv_ref, wo_ref, wq_ref, wmo_ref,
                    gpost1_ref, gpre2_ref, gpost2_ref, o_ref, *, scale):
    mix = jnp.dot(sb_ref[...], wo_ref[0:SB_WIDTH, :], preferred_element_type=F32)
    mix = mix + jnp.dot(cv_ref[...], wo_ref[SB_WIDTH:SB_WIDTH + CONV_WIDTH, :], preferred_element_type=F32)
    mix = mix + jnp.dot(hg_ref[...], wo_ref[SB_WIDTH + CONV_WIDTH:, :], preferred_element_type=F32)
    x = x_ref[...] + _rms(mix, gpost1_ref[...])

    h = _rms(x, gpre2_ref[...]).astype(BF16)
    q = jnp.dot(h, wq_ref[...], preferred_element_type=F32).astype(BF16)
    d = x.shape[1]
    dh = d // MEM_HEADS
    heads = []
    for hd in range(MEM_HEADS):
        cols = slice(hd * dh, (hd + 1) * dh)
        s = lax.dot_general(q[:, cols], mk_ref[:, cols], (((1,), (1,)), ((), ())),
                            preferred_element_type=F32) * scale
        s = s - jnp.max(s, axis=-1, keepdims=True)
        p = jnp.exp(s)
        p = p / jnp.sum(p, axis=-1, keepdims=True)
        heads.append(jnp.dot(p.astype(BF16), mv_ref[:, cols], preferred_element_type=F32))
    att = jnp.concatenate(heads, axis=1).astype(BF16)
    out = jnp.dot(att, wmo_ref[...], preferred_element_type=F32)
    o_ref[...] = x + _rms(out, gpost2_ref[...])


def _out_mem(x, sb, cv, hg, mem_k, mem_v, w_out, w_mq, w_mo, gpost1, gpre2, gpost2, *, tm):
    b_sz, t_len, d = x.shape
    n_mem = mem_k.shape[1]
    tok = lambda width: pl.BlockSpec((None, tm, width), lambda b, t: (b, t, 0))
    per_b = pl.BlockSpec((None, n_mem, d), lambda b, t: (b, 0, 0))
    full = lambda shape: pl.BlockSpec(shape, lambda b, t: (0, 0), pipeline_mode=pl.Buffered(1))
    return pl.pallas_call(
        functools.partial(_out_mem_kernel, scale=1.0 / math.sqrt(d // MEM_HEADS)),
        grid=(b_sz, t_len // tm),
        in_specs=[tok(d), tok(SB_WIDTH), tok(CONV_WIDTH), tok(HG_WIDTH), per_b, per_b,
                  full(w_out.shape), full(w_mq.shape), full(w_mo.shape),
                  full((1, d)), full((1, d)), full((1, d))],
        out_specs=tok(d),
        out_shape=jax.ShapeDtypeStruct((b_sz, t_len, d), F32),
        compiler_params=_params("parallel", "parallel"),
        name="out_mem",
    )(x, sb, cv, hg, mem_k, mem_v, w_out, w_mq, w_mo, gpost1, gpre2, gpost2)


def _in_segments(feature_major):
    return (
        (0, 0, SB_WIDTH, (BF16,), False),
        (0, SB_WIDTH, SB_WIDTH, (F32, BF16), feature_major),
        (0, 2 * SB_WIDTH, SB_WIDTH, (F32, BF16), feature_major),
        (0, 3 * SB_WIDTH, 3 * CONV_WIDTH, (F32,), False),
        (0, 3 * SB_WIDTH + 3 * CONV_WIDTH, 4 * HG_WIDTH, (F32,), False),
    )


def _sb_plan(t_len, q_pos0):
    bq = min(LANES, t_len)
    best = None
    for n_sub in (4, 3, 2, 1):
        width = n_sub * LANES
        if any((q_pos0 + i * bq) % width + bq > width for i in range(t_len // bq)):
            continue
        tk = -(-(q_pos0 + t_len) // width) * width
        if best is None or tk < best[2]:
            best = (bq, n_sub, tk)
    return best


def _state_to_kernel(s):
    b_sz = s.shape[0]
    eye = jnp.eye(HG_HEADS, dtype=s.dtype)
    return jnp.einsum('bhkv,hg->bhvgk', s, eye).reshape(b_sz, HG_WIDTH, HG_WIDTH)


def _state_from_kernel(st):
    b_sz = st.shape[0]
    st5 = st.reshape(b_sz, HG_HEADS, HEAD_DIM, HG_HEADS, HEAD_DIM)
    diag = jnp.stack([st5[:, h, :, h, :] for h in range(HG_HEADS)], axis=1)
    return jnp.swapaxes(diag, -1, -2)


def _trunk(x, q_pos0, sb_k_past, sb_v_past, conv_past, hg_past, mem_k, mem_v, w, consts):
    b_sz, t_len, d = x.shape
    depth = w["w_in"].shape[0]
    n = b_sz * t_len
    tm_ffn = _row_tile(n, 1024)
    tm_tok = _row_tile(t_len, 512)
    in_kernel_t = t_len % LANES == 0
    tm_proj = tm_tok if in_kernel_t else _row_tile(n, 512)
    bq, n_sub, tk = _sb_plan(t_len, q_pos0)
    row = lambda a, l, s: a[l, s][None, :]
    feature_major = lambda a: a if in_kernel_t else a.reshape(b_sz, t_len, SB_WIDTH).transpose(0, 2, 1)

    def keys(past, new):
        parts = []
        if past is not None:
            parts.append(past.transpose(0, 2, 3, 1).reshape(b_sz, SB_WIDTH, q_pos0).astype(BF16))
        parts.append(new)
        if tk > q_pos0 + t_len:
            parts.append(jnp.zeros((b_sz, SB_WIDTH, tk - q_pos0 - t_len), BF16))
        return parts[0] if len(parts) == 1 else jnp.concatenate(parts, axis=2)

    new_k, new_v, new_conv, new_hg = [], [], [], []
    for l in range(depth):
        x2 = _ffn(x.reshape(n, d), row(w["norm_pre"], l, 0), row(w["norm_post"], l, 0),
                  *w["ffn1"][l], tm=tm_ffn)
        sb_q, k_l, k_b, v_l, v_b, cv, hg = _norm_proj(x2, row(w["norm_pre"], l, 1), [w["w_in"][l]],
                                                      _in_segments(in_kernel_t), tm=tm_proj, seq_len=t_len)
        k_l, k_b, v_l, v_b = (feature_major(a) for a in (k_l, k_b, v_l, v_b))
        sb_out = _sb_attention(sb_q.reshape(b_sz, t_len, SB_WIDTH),
                               keys(None if sb_k_past is None else sb_k_past[l], k_b),
                               keys(None if sb_v_past is None else sb_v_past[l], v_b),
                               bq=bq, n_sub=n_sub, q_pos0=q_pos0)
        conv_out, hg_out, conv_state, st = _mixer(
            cv.reshape(b_sz, t_len, -1), hg.reshape(b_sz, t_len, -1), conv_past[l], w["conv_w"][l],
            w["hg_lb"], w["hg_norm"][l][None, :], _state_to_kernel(hg_past[l]), consts,
            layer=l, tt=tm_tok)
        x3 = _out_mem(x2.reshape(b_sz, t_len, d), sb_out, conv_out, hg_out, mem_k[l], mem_v[l],
                      w["w_out"][l], w["w_mq"][l], w["w_mo"][l],
                      row(w["norm_post"], l, 1), row(w["norm_pre"], l, 2), row(w["norm_post"], l, 2),
                      tm=_row_tile(t_len, 1024))
        x = _ffn(x3.reshape(n, d), row(w["norm_pre"], l, 3), row(w["norm_post"], l, 3),
                 *w["ffn2"][l], tm=tm_ffn).reshape(b_sz, t_len, d)
        new_k.append(k_l)
        new_v.append(v_l)
        new_conv.append(conv_state)
        new_hg.append(_state_from_kernel(st))

    def positions_major(layers):
        a = jnp.stack(layers).reshape(depth, b_sz, SB_HEADS, HEAD_DIM, t_len)
        return a.transpose(0, 1, 4, 2, 3)

    return x, positions_major(new_k), positions_major(new_v), jnp.stack(new_conv), jnp.stack(new_hg)


def kernel(x_prompt, x_sample, mem_prompt, cache_sb_k, cache_sb_v, cache_mem_k, cache_mem_v,
           state_conv, state_hgrn, norm_pre, norm_post, ffn1_gu, ffn1_down, w_in, conv_w, hg_lb,
           hg_norm, w_out, mem_norm, w_mk, w_mv, w_mq, w_mo, ffn2_gu, ffn2_down):
    depth = w_in.shape[0]
    b_p, n_mem, d = mem_prompt.shape
    bf = lambda a: a.astype(BF16)
    d_ff = ffn1_down.shape[1]
    tf = FFN_SLAB if d_ff % FFN_SLAB == 0 else d_ff
    w = dict(norm_pre=norm_pre, norm_post=norm_post,
             ffn1=[_ffn_weights(ffn1_gu[l], ffn1_down[l], tf) for l in range(depth)],
             ffn2=[_ffn_weights(ffn2_gu[l], ffn2_down[l], tf) for l in range(depth)],
             w_in=bf(w_in), conv_w=conv_w, hg_lb=hg_lb, hg_norm=hg_norm, w_out=bf(w_out),
             w_mq=bf(w_mq), w_mo=bf(w_mo))
    consts = _mixer_consts()

    mem_flat = mem_prompt.reshape(b_p * n_mem, d)
    w_mk_b, w_mv_b = bf(w_mk), bf(w_mv)
    mem_k_l, mem_v_l, mem_k_b, mem_v_b = [], [], [], []
    for l in range(depth):
        mk, mkb, mv, mvb = _norm_proj(mem_flat, mem_norm[l][None, :], [w_mk_b[l], w_mv_b[l]],
                                      ((0, 0, d, (F32, BF16), False), (1, 0, d, (F32, BF16), False)),
                                      tm=_row_tile(b_p * n_mem, 512))
        mem_k_l.append(mk.reshape(b_p, n_mem, MEM_HEADS, d // MEM_HEADS))
        mem_v_l.append(mv.reshape(b_p, n_mem, MEM_HEADS, d // MEM_HEADS))
        mem_k_b.append(mkb.reshape(b_p, n_mem, d))
        mem_v_b.append(mvb.reshape(b_p, n_mem, d))
    mem_k_p = jnp.stack(mem_k_l)
    mem_v_p = jnp.stack(mem_v_l)

    dt = x_prompt.dtype
    conv0 = jnp.zeros((depth, b_p, CONV_K - 1, CONV_WIDTH), dt)
    hg0 = jnp.zeros((depth, b_p, HG_HEADS, HEAD_DIM, HEAD_DIM), F32)
    y_p, sb_k_p, sb_v_p, conv_p, hg_p = _trunk(
        x_prompt, 0, None, None, conv0, hg0, mem_k_b, mem_v_b, w, consts)
    b_s = x_sample.shape[0]
    cache_k_b = [bf(cache_mem_k[l]).reshape(b_s, -1, d) for l in range(depth)]
    cache_v_b = [bf(cache_mem_v[l]).reshape(b_s, -1, d) for l in range(depth)]
    y_s, sb_k_s, sb_v_s, conv_s, hg_s = _trunk(
        x_sample, cache_sb_k.shape[2], cache_sb_k, cache_sb_v, state_conv, state_hgrn,
        cache_k_b, cache_v_b, w, consts)
    return (y_p, y_s, sb_k_p, sb_v_p, conv_p, hg_p, mem_k_p, mem_v_p, sb_k_s, sb_v_s, conv_s, hg_s)
```

```python
import functools
import math

import numpy as np
import jax
import jax.numpy as jnp
from jax import lax
from jax.experimental import pallas as pl
from jax.experimental.pallas import tpu as pltpu

F32 = jnp.float32
BF16 = jnp.bfloat16

EPS = 1e-6
LOG2_E = math.log2(math.e)
F_FLOOR = 1e-30
HEAD_DIM = 64
SB_HEADS = 8
SB_WIDTH = SB_HEADS * HEAD_DIM
CONV_WIDTH = 256
CONV_K = 3
HG_HEADS = 4
HG_WIDTH = HG_HEADS * HEAD_DIM
MEM_HEADS = 4
CHUNK = 64
SB_DEAD_LOG2 = -150.0
MIXER_CHUNKS_PER_TRIP = 8
FFN_SLABS_PER_TRIP = 3
FFN_SLAB = 256
SB_CHUNK_PREFERENCE = (2, 4, 3, 1)
SB_PAIRS_PER_STEP = 4
LANES = 128
SUBLANES = 8
VMEM_LIMIT = 56 * 1024 * 1024


def _params(*sem):
    return pltpu.CompilerParams(dimension_semantics=sem, vmem_limit_bytes=VMEM_LIMIT)


def _rms(x, g):
    ms = jnp.mean(x * x, axis=-1, keepdims=True)
    return x * lax.rsqrt(ms + EPS) * g


def _split_bf16(x):
    hi = x.astype(BF16)
    lo = (x - hi.astype(F32)).astype(BF16)
    return hi, lo


def _row_tile(n, want):
    t = min(n, want)
    while n % t:
        t //= 2
    return t


def _ffn_kernel(x_ref, gpre_ref, gpost_ref, wgu_ref, wd_ref, o_ref, h_scr, acc_scr):
    nj = wd_ref.shape[0]

    def slab(h, j):
        gate = jnp.dot(h, wgu_ref[0, j], preferred_element_type=F32)
        up = jnp.dot(h, wgu_ref[1, j], preferred_element_type=F32)
        act = (gate * jax.nn.sigmoid(gate) * up).astype(BF16)
        return jnp.dot(act, wd_ref[j], preferred_element_type=F32)

    h = _rms(x_ref[...], gpre_ref[...]).astype(BF16)
    h_scr[...] = h
    acc_scr[...] = slab(h, 0)

    n_mid = max(nj - 2, 0)
    per_trip = math.gcd(n_mid, FFN_SLABS_PER_TRIP) if n_mid else 1

    def middle(i, c):
        part = slab(h_scr[...], 1 + i * per_trip)
        for u in range(1, per_trip):
            part = part + slab(h_scr[...], 1 + i * per_trip + u)
        acc_scr[...] += part
        return c

    lax.fori_loop(0, n_mid // per_trip, middle, 0)
    total = acc_scr[...] + slab(h_scr[...], nj - 1) if nj > 1 else acc_scr[...]
    o_ref[...] = x_ref[...] + 0.5 * _rms(total, gpost_ref[...])


def _ffn_weights(w_gu, w_down, tf):
    d, two_ff = w_gu.shape
    nj = two_ff // 2 // tf
    return (w_gu.astype(BF16).reshape(d, 2, nj, tf).transpose(1, 2, 0, 3),
            w_down.astype(BF16).reshape(nj, tf, d))


def _ffn(x, gpre, gpost, w_gu, w_down, *, tm):
    n, d = x.shape
    resident = lambda a: pl.BlockSpec(a.shape, lambda i: (0,) * a.ndim, pipeline_mode=pl.Buffered(1))
    return pl.pallas_call(
        _ffn_kernel,
        grid=(n // tm,),
        in_specs=[
            pl.BlockSpec((tm, d), lambda i: (i, 0)),
            pl.BlockSpec((1, d), lambda i: (0, 0)),
            pl.BlockSpec((1, d), lambda i: (0, 0)),
            resident(w_gu),
            resident(w_down),
        ],
        out_specs=pl.BlockSpec((tm, d), lambda i: (i, 0)),
        out_shape=jax.ShapeDtypeStruct((n, d), F32),
        scratch_shapes=[pltpu.VMEM((tm, d), BF16), pltpu.VMEM((tm, d), F32)],
        compiler_params=_params("parallel"),
        name="ffn",
    )(x, gpre, gpost, w_gu, w_down)


def _norm_proj_kernel(*refs, n_w, segments):
    x_ref, g_ref = refs[0], refs[1]
    w_refs = refs[2:2 + n_w]
    o_refs = refs[2 + n_w:]
    h = _rms(x_ref[...], g_ref[...]).astype(BF16)
    o_iter = iter(o_refs)
    for wi, off, width, dtypes, transposed in segments:
        res = jnp.dot(h, w_refs[wi][:, off:off + width], preferred_element_type=F32)
        if transposed:
            res = res.T
        for _ in dtypes:
            o_ref = next(o_iter)
            o_ref[...] = res.astype(o_ref.dtype)


def _norm_proj(x, g, weights, segments, *, tm, seq_len=None):
    n, d = x.shape
    in_specs = [pl.BlockSpec((tm, d), lambda i: (i, 0)), pl.BlockSpec((1, d), lambda i: (0, 0))]
    in_specs += [pl.BlockSpec(w.shape, lambda i: (0, 0)) for w in weights]
    out_specs, out_shape = [], []
    for _, _, width, dtypes, transposed in segments:
        for dt in dtypes:
            if transposed:
                tiles = seq_len // tm
                out_specs.append(pl.BlockSpec((None, width, tm), lambda i, tiles=tiles: (i // tiles, 0, i % tiles)))
                out_shape.append(jax.ShapeDtypeStruct((n // seq_len, width, seq_len), dt))
            else:
                out_specs.append(pl.BlockSpec((tm, width), lambda i: (i, 0)))
                out_shape.append(jax.ShapeDtypeStruct((n, width), dt))
    return pl.pallas_call(
        functools.partial(_norm_proj_kernel, n_w=len(weights), segments=tuple(segments)),
        grid=(n // tm,),
        in_specs=in_specs,
        out_specs=out_specs,
        out_shape=out_shape,
        compiler_params=_params("parallel"),
        name="norm_proj",
    )(x, g, *weights)


def _sb_kernel(q_ref, kt_ref, vt_ref, tri_ref, o_ref, acc_scr, carry_scr, z_scr, *, bq, n_sub, n_pair, q_pos0, scale):
    width = n_sub * LANES
    pos0 = q_pos0 + pl.program_id(2) * bq
    sd = (pos0 + bq - 1) // width
    off = pos0 - sd * width
    lane = lax.broadcasted_iota(jnp.int32, (1, LANES), 1)
    even = lane < HEAD_DIM
    heads = [(pp, hh) for pp in range(n_pair) for hh in range(2)]
    pair_cols = lambda pp: slice(pp * LANES, (pp + 1) * LANES)
    log2_scale = scale * LOG2_E

    even_rows = lax.broadcasted_iota(jnp.int32, (LANES, 1), 0) < HEAD_DIM

    def split_pair(x, is_even):
        zero = jnp.zeros_like(x)
        return jnp.where(is_even, x, zero), jnp.where(is_even, zero, x)

    q_heads = [split_pair(q_ref[:, pair_cols(pp)], even) for pp in range(n_pair)]
    acc_scr[...] = jnp.zeros_like(acc_scr)
    carry_scr[...] = jnp.zeros_like(carry_scr)

    def logits(s, nc):
        start = pl.multiple_of(s * width, width)
        out = []
        for pp, hh in heads:
            kb = kt_ref[pair_cols(pp), pl.ds(start, nc * LANES)]
            out.append(jnp.dot(q_heads[pp][hh], kb, preferred_element_type=F32) * log2_scale)
        return out

    def weigh(s, zs, nc, last_off):
        start = pl.multiple_of(s * width, width)
        if last_off is not None:
            row = lax.broadcasted_iota(jnp.int32, (bq, LANES), 0)
            col = lax.broadcasted_iota(jnp.int32, (bq, LANES), 1)
            strict = col < row + last_off
        for pp in range(n_pair):
            v_heads = split_pair(vt_ref[pair_cols(pp), pl.ds(start, nc * LANES)], even_rows)
            out = acc_scr[:, pair_cols(pp)]
            for hh in range(2):
                h = 2 * pp + hh
                z = zs[h]
                mn = jnp.minimum(z, 0.0)
                nz = mn - z
                sp = jnp.log2(1.0 + jnp.exp2(mn + nz))
                log_beta = mn - sp
                log_keep = nz - sp
                carry = carry_scr[h]
                ws = [None] * nc
                for c in reversed(range(nc)):
                    cols = slice(c * LANES, (c + 1) * LANES)
                    masked = last_off is not None and c == nc - 1
                    lk = jnp.where(strict, log_keep[:, cols], 0.0) if masked else log_keep[:, cols]
                    hi, lo = _split_bf16(lk)
                    sums = jnp.dot(jnp.concatenate([hi, lo], axis=1), tri_ref[...], preferred_element_type=F32)
                    arg = log_beta[:, cols] + sums[:, :LANES] + carry
                    if masked:
                        w = jnp.where(strict, jnp.exp2(jnp.where(strict, arg, 0.0)), 0.0)
                    else:
                        w = jnp.exp2(arg)
                    ws[c] = w.astype(BF16)
                    carry = carry + sums[:, LANES:]
                carry_scr[h] = carry
                w_all = ws[0] if nc == 1 else jnp.concatenate(ws, axis=1)
                out = out + lax.dot_general(w_all, v_heads[hh], (((1,), (1,)), ((), ())),
                                            preferred_element_type=F32)
            acc_scr[:, pair_cols(pp)] = out

    def prefetch_logits(s):
        zs = logits(jnp.maximum(s, 0), n_sub)
        for h in range(len(heads)):
            z_scr[h] = zs[h]

    def diagonal(nc):
        def run():
            weigh(sd, logits(sd, nc), nc, off - (nc - 1) * LANES)
            prefetch_logits(sd - 1)
        return run

    lax.switch((off + bq - 1) // LANES, [diagonal(nc) for nc in range(1, n_sub + 1)])

    def more_to_do(blocks_left):
        live = jnp.max(carry_scr[...]) > SB_DEAD_LOG2
        return jnp.logical_and(blocks_left > 0, live).astype(jnp.int32)

    def body(state):
        it, _ = state
        s = sd - 1 - it
        zs = [z_scr[h] for h in range(len(heads))]
        prefetch_logits(s - 1)
        weigh(s, zs, n_sub, None)
        return it + 1, more_to_do(s)

    lax.while_loop(lambda state: state[1] != 0, body, (jnp.int32(0), more_to_do(sd)))
    o_ref[...] = acc_scr[...].astype(o_ref.dtype)


def _sb_tri():
    s_from = np.arange(LANES)[:, None]
    s_to = np.arange(LANES)[None, :]
    half = np.concatenate([(s_from > s_to), np.ones((LANES, LANES), bool)], axis=1)
    return jnp.asarray(np.concatenate([half, half], axis=0), dtype=BF16)


def _sb_attention(q, kt, vt, *, bq, n_sub, q_pos0):
    b_sz, tq, _ = q.shape
    tk = kt.shape[2]
    width = n_sub * LANES
    assert tq % bq == 0 and tk % width == 0 and tk >= q_pos0 + tq
    assert all((q_pos0 + i * bq) % width + bq <= width for i in range(tq // bq))
    n_pair = SB_PAIRS_PER_STEP
    cols = n_pair * LANES
    kern = functools.partial(_sb_kernel, bq=bq, n_sub=n_sub, n_pair=n_pair, q_pos0=q_pos0,
                             scale=1.0 / math.sqrt(HEAD_DIM))
    return pl.pallas_call(
        kern,
        grid=(b_sz, SB_WIDTH // cols, tq // bq),
        in_specs=[
            pl.BlockSpec((None, bq, cols), lambda b, p, i: (b, i, p)),
            pl.BlockSpec((None, cols, tk), lambda b, p, i: (b, p, 0)),
            pl.BlockSpec((None, cols, tk), lambda b, p, i: (b, p, 0)),
            pl.BlockSpec((2 * LANES, 2 * LANES), lambda b, p, i: (0, 0)),
        ],
        out_specs=pl.BlockSpec((None, bq, cols), lambda b, p, i: (b, i, p)),
        out_shape=jax.ShapeDtypeStruct((b_sz, tq, SB_WIDTH), BF16),
        scratch_shapes=[pltpu.VMEM((bq, cols), F32), pltpu.VMEM((2 * n_pair, bq, LANES), F32),
                        pltpu.VMEM((2 * n_pair, bq, width), F32)],
        compiler_params=_params("parallel", "parallel", "arbitrary"),
        name="sb_attention",
    )(q, kt, vt, _sb_tri())


def _tri_rows():
    offs, sizes = [], []
    off = 0
    for t in range(CHUNK):
        n = SUBLANES * (t // SUBLANES + 1)
        offs.append(off)
        sizes.append(n)
        off += n
    return offs, sizes, off


_TRI_OFFS, _TRI_SIZES, _TRI_ROWS = _tri_rows()


def _mixer_consts():
    head = np.arange(HG_WIDTH) // HEAD_DIM
    same_head = (head[:, None] == head[None, :])
    sel = np.zeros((CHUNK, CHUNK * SUBLANES), bool)
    for t in range(CHUNK):
        sel[t, t * SUBLANES:(t + 1) * SUBLANES] = True
    incl = np.arange(CHUNK)[:, None] >= np.arange(CHUNK)[None, :]
    cum = np.concatenate([incl, incl], axis=1)
    return (jnp.asarray(same_head, dtype=BF16), jnp.asarray(same_head, dtype=F32),
            jnp.asarray(sel, dtype=BF16), jnp.asarray(cum, dtype=BF16))


def _mixer_kernel(cv_ref, hg_ref, cpast_ref, cw_ref, lbl_ref, gn_ref, st0_ref,
                  bd_ref, bdf_ref, sel_ref, cum_ref,
                  conv_o_ref, hg_o_ref, cstate_ref, st_ref,
                  u_scr, b_scr, k_scr, q_scr, a_scr, o_scr, *, layer, tt):
    ti = pl.program_id(1)
    last = pl.num_programs(1) - 1
    pad = SUBLANES

    @pl.when(ti == 0)
    def _():
        u_scr[pad - 2:pad, :] = cpast_ref[...]
        st_ref[...] = st0_ref[...]

    cb = cv_ref[:, 0:CONV_WIDTH]
    u_scr[pad:pad + tt, :] = cv_ref[:, CONV_WIDTH:2 * CONV_WIDTH] * cv_ref[:, 2 * CONV_WIDTH:3 * CONV_WIDTH]
    y = u_scr[pad - 2:pad - 2 + tt, :] * cw_ref[0:1, :]
    y = y + u_scr[pad - 1:pad - 1 + tt, :] * cw_ref[1:2, :]
    y = y + u_scr[pad:pad + tt, :] * cw_ref[2:3, :]
    conv_o_ref[...] = (cb * y).astype(conv_o_ref.dtype)
    tail = u_scr[pad + tt - 2:pad + tt, :]
    u_scr[pad - 2:pad, :] = tail

    @pl.when(ti == last)
    def _():
        cstate_ref[...] = tail

    logits = lbl_ref[...]
    mx = jnp.max(logits, axis=0, keepdims=True)
    ex = jnp.exp(logits - mx)
    den = jnp.sum(ex, axis=0, keepdims=True)
    lb = jnp.zeros_like(den)
    for i in range(1, layer + 1):
        lb = lb + ex[i:i + 1, :] / den
    one_m_lb = 1.0 - lb

    sub_row = lax.broadcasted_iota(jnp.int32, (SUBLANES, HG_WIDTH), 0)

    def chunk_rows(c):
        return pl.ds(pl.multiple_of(c * CHUNK, CHUNK), CHUNK)

    def gates(c, par):
        rows = chunk_rows(c)
        qh = hg_ref[rows, 0:HG_WIDTH]
        a = hg_ref[rows, HG_WIDTH:2 * HG_WIDTH]
        vi = hg_ref[rows, 2 * HG_WIDTH:3 * HG_WIDTH]
        e = jnp.exp(-jnp.abs(a))
        r = 1.0 / (1.0 + e)
        pos = a >= 0.0
        sig = jnp.where(pos, r, e * r)
        nsig = jnp.where(pos, e * r, r)
        f = lb + one_m_lb * sig
        g = jnp.log(jnp.maximum(f, F_FLOOR)) * LOG2_E
        kk = one_m_lb * nsig
        g_hi, g_lo = _split_bf16(g)
        b = jnp.dot(cum_ref[...], jnp.concatenate([g_hi, g_lo], axis=0),
                    preferred_element_type=F32)
        b_scr[par] = b
        k_scr[par] = kk
        q_scr[par] = qh
        st = st_ref[...]
        inter = lax.dot_general((qh * jnp.exp2(b)).astype(BF16), st.astype(BF16),
                                (((1,), (1,)), ((), ())), preferred_element_type=F32)
        b_last = b[CHUNK - 1:CHUNK, :]
        kd = kk * jnp.exp2(b_last - b)
        upd = lax.dot_general(vi.astype(BF16), kd.astype(BF16), (((0,), (0,)), ((), ())),
                              preferred_element_type=F32)
        st_ref[...] = st * jnp.exp2(b_last) + upd * bdf_ref[...]
        o_scr[rows, :] = inter

    def intra(c, par):
        b_buf, k_buf, q_buf, a_buf = b_scr.at[par], k_scr.at[par], q_scr.at[par], a_scr.at[par]
        rows = chunk_rows(c)
        vi = hg_ref[rows, 2 * HG_WIDTH:3 * HG_WIDTH]
        for t in range(CHUNK):
            n = _TRI_SIZES[t]
            qt = q_buf[t:t + 1, :]
            bt = b_buf[t:t + 1, :]
            diff = bt - b_buf[0:n, :]
            a_full = qt * jnp.exp2(diff[:n - SUBLANES]) * k_buf[0:n - SUBLANES, :] if n > SUBLANES else None
            ok = sub_row <= (t % SUBLANES)
            d_last = jnp.where(ok, diff[n - SUBLANES:], 0.0)
            a_last = jnp.where(ok, qt * jnp.exp2(d_last) * k_buf[n - SUBLANES:n, :], 0.0)
            if a_full is not None:
                a_buf[_TRI_OFFS[t]:_TRI_OFFS[t] + n - SUBLANES, :] = a_full
            a_buf[_TRI_OFFS[t] + n - SUBLANES:_TRI_OFFS[t] + n, :] = a_last
        scores = jnp.dot(a_buf[...].astype(BF16), bd_ref[...], preferred_element_type=F32)
        partial = []
        for t in range(CHUNK):
            off = _TRI_OFFS[t]
            tile = scores[off:off + SUBLANES] * vi[0:SUBLANES]
            for s0 in range(SUBLANES, _TRI_SIZES[t], SUBLANES):
                tile = tile + scores[off + s0:off + s0 + SUBLANES] * vi[s0:s0 + SUBLANES]
            partial.append(tile)
        o_scr[rows, :] += jnp.dot(sel_ref[...], jnp.concatenate(partial, axis=0).astype(BF16),
                                  preferred_element_type=F32)

    n_chunks = tt // CHUNK
    per_trip = math.gcd(n_chunks, MIXER_CHUNKS_PER_TRIP)

    def trip(i, carry):
        first = i * per_trip
        gates(first, 0)
        for u in range(per_trip):
            if u + 1 < per_trip:
                gates(first + u + 1, (u + 1) % 2)
            intra(first + u, u % 2)
        return carry

    lax.fori_loop(0, n_chunks // per_trip, trip, 0)

    o = o_scr[...]
    gg = hg_ref[:, 3 * HG_WIDTH:4 * HG_WIDTH]
    o_hi, o_lo = _split_bf16(o * o)
    ms = (jnp.dot(o_hi, bd_ref[...], preferred_element_type=F32)
          + jnp.dot(o_lo, bd_ref[...], preferred_element_type=F32)) * (1.0 / HEAD_DIM)
    yo = o * lax.rsqrt(ms + EPS) * gn_ref[...] * (gg * jax.nn.sigmoid(gg))
    hg_o_ref[...] = yo.astype(hg_o_ref.dtype)


def _mixer(cv, hg, conv_past, conv_w, lb_logits, hg_norm, st0, consts, *, layer, tt):
    b_sz, t_len, _ = cv.shape
    assert t_len % tt == 0 and tt % CHUNK == 0
    bd, bdf, sel, cum = consts
    depth = lb_logits.shape[0]
    const2 = lambda b, t: (0, 0)
    return pl.pallas_call(
        functools.partial(_mixer_kernel, layer=layer, tt=tt),
        grid=(b_sz, t_len // tt),
        in_specs=[
            pl.BlockSpec((None, tt, 3 * CONV_WIDTH), lambda b, t: (b, t, 0)),
            pl.BlockSpec((None, tt, 4 * HG_WIDTH), lambda b, t: (b, t, 0)),
            pl.BlockSpec((None, CONV_K - 1, CONV_WIDTH), lambda b, t: (b, 0, 0)),
            pl.BlockSpec((CONV_K, CONV_WIDTH), const2),
            pl.BlockSpec((depth, HG_WIDTH), const2),
            pl.BlockSpec((1, HG_WIDTH), const2),
            pl.BlockSpec((None, HG_WIDTH, HG_WIDTH), lambda b, t: (b, 0, 0)),
            pl.BlockSpec(bd.shape, const2),
            pl.BlockSpec(bdf.shape, const2),
            pl.BlockSpec(sel.shape, const2),
            pl.BlockSpec(cum.shape, const2),
        ],
        out_specs=[
            pl.BlockSpec((None, tt, CONV_WIDTH), lambda b, t: (b, t, 0)),
            pl.BlockSpec((None, tt, HG_WIDTH), lambda b, t: (b, t, 0)),
            pl.BlockSpec((None, CONV_K - 1, CONV_WIDTH), lambda b, t: (b, 0, 0)),
            pl.BlockSpec((None, HG_WIDTH, HG_WIDTH), lambda b, t: (b, 0, 0)),
        ],
        out_shape=[
            jax.ShapeDtypeStruct((b_sz, t_len, CONV_WIDTH), BF16),
            jax.ShapeDtypeStruct((b_sz, t_len, HG_WIDTH), BF16),
            jax.ShapeDtypeStruct((b_sz, CONV_K - 1, CONV_WIDTH), F32),
            jax.ShapeDtypeStruct((b_sz, HG_WIDTH, HG_WIDTH), F32),
        ],
        scratch_shapes=[
            pltpu.VMEM((tt + SUBLANES, CONV_WIDTH), F32),
            pltpu.VMEM((2, CHUNK, HG_WIDTH), F32),
            pltpu.VMEM((2, CHUNK, HG_WIDTH), F32),
            pltpu.VMEM((2, CHUNK, HG_WIDTH), F32),
            pltpu.VMEM((2, _TRI_ROWS, HG_WIDTH), F32),
            pltpu.VMEM((tt, HG_WIDTH), F32),
        ],
        compiler_params=_params("parallel", "arbitrary"),
        name="mixer",
    )(cv, hg, conv_past, conv_w, lb_logits, hg_norm, st0, bd, bdf, sel, cum)


def _out_mem_kernel(x_ref, sb_ref, cv_ref, hg_ref, mk_ref, mv_ref, wo_ref, wq_ref, wmo_ref,
                    gpost1_ref, gpre2_ref, gpost2_ref, o_ref, *, scale):
    mix = jnp.dot(sb_ref[...], wo_ref[0:SB_WIDTH, :], preferred_element_type=F32)
    mix = mix + jnp.dot(cv_ref[...], wo_ref[SB_WIDTH:SB_WIDTH + CONV_WIDTH, :], preferred_element_type=F32)
    mix = mix + jnp.dot(hg_ref[...], wo_ref[SB_WIDTH + CONV_WIDTH:, :], preferred_element_type=F32)
    x = x_ref[...] + _rms(mix, gpost1_ref[...])

    h = _rms(x, gpre2_ref[...]).astype(BF16)
    q = jnp.dot(h, wq_ref[...], preferred_element_type=F32).astype(BF16)
    d = x.shape[1]
    dh = d // MEM_HEADS
    heads = []
    for hd in range(MEM_HEADS):
        cols = slice(hd * dh, (hd + 1) * dh)
        s = lax.dot_general(q[:, cols], mk_ref[:, cols], (((1,), (1,)), ((), ())),
                            preferred_element_type=F32) * scale
        s = s - jnp.max(s, axis=-1, keepdims=True)
        p = jnp.exp(s)
        p = p / jnp.sum(p, axis=-1, keepdims=True)
        heads.append(jnp.dot(p.astype(BF16), mv_ref[:, cols], preferred_element_type=F32))
    att = jnp.concatenate(heads, axis=1).astype(BF16)
    out = jnp.dot(att, wmo_ref[...], preferred_element_type=F32)
    o_ref[...] = x + _rms(out, gpost2_ref[...])


def _out_mem(x, sb, cv, hg, mem_k, mem_v, w_out, w_mq, w_mo, gpost1, gpre2, gpost2, *, tm):
    b_sz, t_len, d = x.shape
    n_mem = mem_k.shape[1]
    tok = lambda width: pl.BlockSpec((None, tm, width), lambda b, t: (b, t, 0))
    per_b = pl.BlockSpec((None, n_mem, d), lambda b, t: (b, 0, 0))
    full = lambda shape: pl.BlockSpec(shape, lambda b, t: (0, 0), pipeline_mode=pl.Buffered(1))
    return pl.pallas_call(
        functools.partial(_out_mem_kernel, scale=1.0 / math.sqrt(d // MEM_HEADS)),
        grid=(b_sz, t_len // tm),
        in_specs=[tok(d), tok(SB_WIDTH), tok(CONV_WIDTH), tok(HG_WIDTH), per_b, per_b,
                  full(w_out.shape), full(w_mq.shape), full(w_mo.shape),
                  full((1, d)), full((1, d)), full((1, d))],
        out_specs=tok(d),
        out_shape=jax.ShapeDtypeStruct((b_sz, t_len, d), F32),
        compiler_params=_params("parallel", "parallel"),
        name="out_mem",
    )(x, sb, cv, hg, mem_k, mem_v, w_out, w_mq, w_mo, gpost1, gpre2, gpost2)


def _in_segments(feature_major):
    return (
        (0, 0, SB_WIDTH, (BF16,), False),
        (0, SB_WIDTH, SB_WIDTH, (F32, BF16), feature_major),
        (0, 2 * SB_WIDTH, SB_WIDTH, (F32, BF16), feature_major),
        (0, 3 * SB_WIDTH, 3 * CONV_WIDTH, (F32,), False),
        (0, 3 * SB_WIDTH + 3 * CONV_WIDTH, 4 * HG_WIDTH, (F32,), False),
    )


def _sb_plan(t_len, q_pos0):
    bq = min(LANES, t_len)
    best = None
    for n_sub in SB_CHUNK_PREFERENCE:
        width = n_sub * LANES
        if any((q_pos0 + i * bq) % width + bq > width for i in range(t_len // bq)):
            continue
        tk = -(-(q_pos0 + t_len) // width) * width
        if best is None or tk < best[2]:
            best = (bq, n_sub, tk)
    return best


def _state_to_kernel(s):
    b_sz = s.shape[0]
    eye = jnp.eye(HG_HEADS, dtype=s.dtype)
    return jnp.einsum('bhkv,hg->bhvgk', s, eye).reshape(b_sz, HG_WIDTH, HG_WIDTH)


def _state_from_kernel(st):
    b_sz = st.shape[0]
    st5 = st.reshape(b_sz, HG_HEADS, HEAD_DIM, HG_HEADS, HEAD_DIM)
    diag = jnp.stack([st5[:, h, :, h, :] for h in range(HG_HEADS)], axis=1)
    return jnp.swapaxes(diag, -1, -2)


def _trunk(x, q_pos0, sb_k_past, sb_v_past, conv_past, hg_past, mem_k, mem_v, w, consts):
    b_sz, t_len, d = x.shape
    depth = w["w_in"].shape[0]
    n = b_sz * t_len
    tm_ffn = _row_tile(n, 1024)
    tm_tok = _row_tile(t_len, 512)
    in_kernel_t = t_len % LANES == 0
    tm_proj = tm_tok if in_kernel_t else _row_tile(n, 512)
    bq, n_sub, tk = _sb_plan(t_len, q_pos0)
    row = lambda a, l, s: a[l, s][None, :]
    feature_major = lambda a: a if in_kernel_t else a.reshape(b_sz, t_len, SB_WIDTH).transpose(0, 2, 1)

    def keys(past, new):
        parts = []
        if past is not None:
            parts.append(past.transpose(0, 2, 3, 1).reshape(b_sz, SB_WIDTH, q_pos0).astype(BF16))
        parts.append(new)
        if tk > q_pos0 + t_len:
            parts.append(jnp.zeros((b_sz, SB_WIDTH, tk - q_pos0 - t_len), BF16))
        return parts[0] if len(parts) == 1 else jnp.concatenate(parts, axis=2)

    new_k, new_v, new_conv, new_hg = [], [], [], []
    for l in range(depth):
        x2 = _ffn(x.reshape(n, d), row(w["norm_pre"], l, 0), row(w["norm_post"], l, 0),
                  *w["ffn1"][l], tm=tm_ffn)
        sb_q, k_l, k_b, v_l, v_b, cv, hg = _norm_proj(x2, row(w["norm_pre"], l, 1), [w["w_in"][l]],
                                                      _in_segments(in_kernel_t), tm=tm_proj, seq_len=t_len)
        k_l, k_b, v_l, v_b = (feature_major(a) for a in (k_l, k_b, v_l, v_b))
        sb_out = _sb_attention(sb_q.reshape(b_sz, t_len, SB_WIDTH),
                               keys(None if sb_k_past is None else sb_k_past[l], k_b),
                               keys(None if sb_v_past is None else sb_v_past[l], v_b),
                               bq=bq, n_sub=n_sub, q_pos0=q_pos0)
        conv_out, hg_out, conv_state, st = _mixer(
            cv.reshape(b_sz, t_len, -1), hg.reshape(b_sz, t_len, -1), conv_past[l], w["conv_w"][l],
            w["hg_lb"], w["hg_norm"][l][None, :], _state_to_kernel(hg_past[l]), consts,
            layer=l, tt=tm_tok)
        x3 = _out_mem(x2.reshape(b_sz, t_len, d), sb_out, conv_out, hg_out, mem_k[l], mem_v[l],
                      w["w_out"][l], w["w_mq"][l], w["w_mo"][l],
                      row(w["norm_post"], l, 1), row(w["norm_pre"], l, 2), row(w["norm_post"], l, 2),
                      tm=_row_tile(t_len, 1024))
        x = _ffn(x3.reshape(n, d), row(w["norm_pre"], l, 3), row(w["norm_post"], l, 3),
                 *w["ffn2"][l], tm=tm_ffn).reshape(b_sz, t_len, d)
        new_k.append(k_l)
        new_v.append(v_l)
        new_conv.append(conv_state)
        new_hg.append(_state_from_kernel(st))

    def positions_major(layers):
        a = jnp.stack(layers).reshape(depth, b_sz, SB_HEADS, HEAD_DIM, t_len)
        return a.transpose(0, 1, 4, 2, 3)

    return x, positions_major(new_k), positions_major(new_v), jnp.stack(new_conv), jnp.stack(new_hg)


def kernel(x_prompt, x_sample, mem_prompt, cache_sb_k, cache_sb_v, cache_mem_k, cache_mem_v,
           state_conv, state_hgrn, norm_pre, norm_post, ffn1_gu, ffn1_down, w_in, conv_w, hg_lb,
           hg_norm, w_out, mem_norm, w_mk, w_mv, w_mq, w_mo, ffn2_gu, ffn2_down):
    depth = w_in.shape[0]
    b_p, n_mem, d = mem_prompt.shape
    bf = lambda a: a.astype(BF16)
    d_ff = ffn1_down.shape[1]
    tf = FFN_SLAB if d_ff % FFN_SLAB == 0 else d_ff
    w = dict(norm_pre=norm_pre, norm_post=norm_post,
             ffn1=[_ffn_weights(ffn1_gu[l], ffn1_down[l], tf) for l in range(depth)],
             ffn2=[_ffn_weights(ffn2_gu[l], ffn2_down[l], tf) for l in range(depth)],
             w_in=bf(w_in), conv_w=conv_w, hg_lb=hg_lb, hg_norm=hg_norm, w_out=bf(w_out),
             w_mq=bf(w_mq), w_mo=bf(w_mo))
    consts = _mixer_consts()

    mem_flat = mem_prompt.reshape(b_p * n_mem, d)
    w_mk_b, w_mv_b = bf(w_mk), bf(w_mv)
    mem_k_l, mem_v_l, mem_k_b, mem_v_b = [], [], [], []
    for l in range(depth):
        mk, mkb, mv, mvb = _norm_proj(mem_flat, mem_norm[l][None, :], [w_mk_b[l], w_mv_b[l]],
                                      ((0, 0, d, (F32, BF16), False), (1, 0, d, (F32, BF16), False)),
                                      tm=_row_tile(b_p * n_mem, 512))
        mem_k_l.append(mk.reshape(b_p, n_mem, MEM_HEADS, d // MEM_HEADS))
        mem_v_l.append(mv.reshape(b_p, n_mem, MEM_HEADS, d // MEM_HEADS))
        mem_k_b.append(mkb.reshape(b_p, n_mem, d))
        mem_v_b.append(mvb.reshape(b_p, n_mem, d))
    mem_k_p = jnp.stack(mem_k_l)
    mem_v_p = jnp.stack(mem_v_l)

    dt = x_prompt.dtype
    conv0 = jnp.zeros((depth, b_p, CONV_K - 1, CONV_WIDTH), dt)
    hg0 = jnp.zeros((depth, b_p, HG_HEADS, HEAD_DIM, HEAD_DIM), F32)
    y_p, sb_k_p, sb_v_p, conv_p, hg_p = _trunk(
        x_prompt, 0, None, None, conv0, hg0, mem_k_b, mem_v_b, w, consts)
    b_s = x_sample.shape[0]
    cache_k_b = [bf(cache_mem_k[l]).reshape(b_s, -1, d) for l in range(depth)]
    cache_v_b = [bf(cache_mem_v[l]).reshape(b_s, -1, d) for l in range(depth)]
    y_s, sb_k_s, sb_v_s, conv_s, hg_s = _trunk(
        x_sample, cache_sb_k.shape[2], cache_sb_k, cache_sb_v, state_conv, state_hgrn,
        cache_k_b, cache_v_b, w, consts)
    return (y_p, y_s, sb_k_p, sb_v_p, conv_p, hg_p, mem_k_p, mem_v_p, sb_k_s, sb_v_s, conv_s, hg_s)
```
